```python
import math
import jax, jax.numpy as jnp
from jax import lax
import numpy as np

D_MODEL = 1024
BATCH = 4
SEQ = 8192
DEPTH = 2

CHUNK = 64
N_HEADS_DN = 8
HEAD_DIM_DN = 128
KEY_WIDTH = N_HEADS_DN * HEAD_DIM_DN
SHORT_CONV = 4
CONV_CH = D_MODEL
DW_WIDTH = 31
D_FF = 2816
IN_COLS = 4 * KEY_WIDTH + 2 * N_HEADS_DN + 2 * CONV_CH + 2 * D_MODEL
EPS = 1e-6

kernel_name = 'hybrid_gdn_conformer_macaron_trunk'


def rmsnorm(x, w):
    xf = x.astype(jnp.float32)
    y = xf * lax.rsqrt(jnp.mean(xf * xf, axis=-1, keepdims=True) + EPS)
    return (y * w.astype(jnp.float32)).astype(x.dtype)


def l2norm(x):
    xf = x.astype(jnp.float32)
    return xf * lax.rsqrt(jnp.sum(xf * xf, axis=-1, keepdims=True) + EPS)


def swiglu(x, w_gate, w_up, w_down):
    return (jax.nn.silu(x @ w_gate) * (x @ w_up)) @ w_down


def causal_depthwise_conv(x, w):
    width = w.shape[0]
    return lax.conv_general_dilated(
        x, w[:, None, :].astype(x.dtype), window_strides=(1,),
        padding=[(width - 1, 0)], dimension_numbers=('NWC', 'WIO', 'NWC'),
        feature_group_count=x.shape[-1])


def gated_delta_chunked(q, k, v, g, beta):
    bsz, seq, heads, dk = q.shape
    dv = v.shape[-1]
    n_chunks = seq // CHUNK

    def to_chunks(t):
        return t.reshape(bsz, n_chunks, CHUNK, heads, -1).transpose(0, 3, 1, 2, 4)

    q, k, v = to_chunks(q), to_chunks(k), to_chunks(v)
    g = g.reshape(bsz, n_chunks, CHUNK, heads).transpose(0, 3, 1, 2)
    beta = beta.reshape(bsz, n_chunks, CHUNK, heads).transpose(0, 3, 1, 2)
    gc = jnp.cumsum(g, axis=-1)
    idx = jnp.arange(CHUNK)
    strict = idx[:, None] > idx[None, :]
    incl = idx[:, None] >= idx[None, :]
    diff = gc[..., :, None] - gc[..., None, :]
    decay_strict = jnp.exp(jnp.where(strict, diff, -jnp.inf))
    decay_incl = jnp.exp(jnp.where(incl, diff, -jnp.inf))

    kb = k * beta[..., None]
    m = jnp.einsum('bhnid,bhnjd->bhnij', kb, k) * decay_strict
    eye = jnp.eye(CHUNK, dtype=m.dtype)
    rhs = jnp.concatenate([v * beta[..., None], kb * jnp.exp(gc)[..., None]], axis=-1)
    sol = lax.linalg.triangular_solve(eye + m, rhs, left_side=True, lower=True)
    u, w = sol[..., :dv], sol[..., dv:]

    attn = jnp.einsum('bhnid,bhnjd->bhnij', q, k) * decay_incl
    g_last = gc[..., -1]
    qd = q * jnp.exp(gc)[..., None]
    kd = k * jnp.exp(g_last[..., None] - gc)[..., None]
    xs = tuple(jnp.moveaxis(t, 2, 0) for t in (u, w, qd, kd, attn, jnp.exp(g_last)))

    def step(state, inp):
        u_c, w_c, qd_c, kd_c, a_c, eg = inp
        v_new = u_c - jnp.einsum('bhck,bhkv->bhcv', w_c, state)
        o = jnp.einsum('bhck,bhkv->bhcv', qd_c, state) + jnp.einsum('bhij,bhjv->bhiv', a_c, v_new)
        state = state * eg[..., None, None] + jnp.einsum('bhck,bhcv->bhkv', kd_c, v_new)
        return state, o

    s0 = jnp.zeros((bsz, heads, dk, dv), jnp.float32)
    _, o = lax.scan(step, s0, xs)
    return o.transpose(1, 0, 3, 2, 4).reshape(bsz, seq, heads, dv)


def hybrid_mixer(h, w_in, qkv_conv_w, a_log, dt_bias, head_norm, w_a_out,
                 dw_conv_w, dw_conv_b, conv_norm, w_b_out, w_o):
    bsz, seq, _ = h.shape
    proj = h @ w_in
    bounds = np.cumsum([3 * KEY_WIDTH, KEY_WIDTH, N_HEADS_DN, N_HEADS_DN, 2 * CONV_CH]).tolist()
    qkv, g_out, a, b, glu, gates = jnp.split(proj, bounds, axis=-1)

    qkv = jax.nn.silu(causal_depthwise_conv(qkv, qkv_conv_w))
    q, k, v = jnp.split(qkv, 3, axis=-1)
    shp = (bsz, seq, N_HEADS_DN, HEAD_DIM_DN)
    q = l2norm(q.reshape(shp)) * (HEAD_DIM_DN ** -0.5)
    k = l2norm(k.reshape(shp))
    v = v.reshape(shp).astype(jnp.float32)
    beta = jax.nn.sigmoid(b.astype(jnp.float32))
    g = -jnp.exp(a_log.astype(jnp.float32)) * jax.nn.softplus(
        a.astype(jnp.float32) + dt_bias.astype(jnp.float32))
    o = gated_delta_chunked(q, k, v, g, beta)
    o = rmsnorm(o, head_norm) * jax.nn.silu(g_out.reshape(shp).astype(jnp.float32))
    y_a = o.reshape(bsz, seq, KEY_WIDTH).astype(h.dtype) @ w_a_out

    val, gt = jnp.split(glu, 2, axis=-1)
    c = val * jax.nn.sigmoid(gt)
    c = causal_depthwise_conv(c, dw_conv_w) + dw_conv_b
    c = jax.nn.silu(rmsnorm(c, conv_norm))
    y_b = c @ w_b_out

    gate_a, gate_b = jnp.split(gates, 2, axis=-1)
    merged = jax.nn.sigmoid(gate_a) * y_a + jax.nn.sigmoid(gate_b) * y_b
    return merged @ w_o


def setup_inputs(seed: int = 0) -> dict:
    key = jax.random.key(seed)
    ks = jax.random.split(key, 24)

    def nrm(k, shape, scale):
        return jax.random.normal(k, shape, jnp.float32) * scale

    def gain(k, shape):
        return 1.0 + 0.02 * jax.random.normal(k, shape, jnp.float32)

    dt = jnp.exp(jax.random.uniform(ks[9], (DEPTH, N_HEADS_DN), jnp.float32,
                                    minval=float(np.log(1e-3)), maxval=float(np.log(1e-1))))
    return {
        'x': nrm(ks[0], (BATCH, SEQ, D_MODEL), 1.0),
        'ffn1_norm': gain(ks[1], (DEPTH, D_MODEL)),
        'ffn1_w_gate': nrm(ks[2], (DEPTH, D_MODEL, D_FF), D_MODEL ** -0.5),
        'ffn1_w_up': nrm(ks[3], (DEPTH, D_MODEL, D_FF), D_MODEL ** -0.5),
        'ffn1_w_down': nrm(ks[4], (DEPTH, D_FF, D_MODEL), D_FF ** -0.5),
        'mix_norm': gain(ks[5], (DEPTH, D_MODEL)),
        'w_in': nrm(ks[6], (DEPTH, D_MODEL, IN_COLS), D_MODEL ** -0.5),
        'qkv_conv_w': nrm(ks[7], (DEPTH, SHORT_CONV, 3 * KEY_WIDTH), SHORT_CONV ** -0.5),
        'a_log': jnp.log(jax.random.uniform(ks[8], (DEPTH, N_HEADS_DN), jnp.float32,
                                            minval=1.0, maxval=16.0)),
        'dt_bias': dt + jnp.log(-jnp.expm1(-dt)),
        'head_norm': gain(ks[10], (DEPTH, HEAD_DIM_DN)),
        'w_a_out': nrm(ks[11], (DEPTH, KEY_WIDTH, D_MODEL), KEY_WIDTH ** -0.5),
        'dw_conv_w': nrm(ks[12], (DEPTH, DW_WIDTH, CONV_CH), DW_WIDTH ** -0.5),
        'dw_conv_b': nrm(ks[13], (DEPTH, CONV_CH), 0.02),
        'conv_norm': gain(ks[14], (DEPTH, CONV_CH)),
        'w_b_out': nrm(ks[15], (DEPTH, CONV_CH, D_MODEL), CONV_CH ** -0.5),
        'w_o': nrm(ks[16], (DEPTH, D_MODEL, D_MODEL), D_MODEL ** -0.5),
        'ffn2_norm': gain(ks[17], (DEPTH, D_MODEL)),
        'ffn2_w_gate': nrm(ks[18], (DEPTH, D_MODEL, D_FF), D_MODEL ** -0.5),
        'ffn2_w_up': nrm(ks[19], (DEPTH, D_MODEL, D_FF), D_MODEL ** -0.5),
        'ffn2_w_down': nrm(ks[20], (DEPTH, D_FF, D_MODEL), D_FF ** -0.5),
        'final_norm': gain(ks[21], (D_MODEL,)),
    }


def reference(x, ffn1_norm, ffn1_w_gate, ffn1_w_up, ffn1_w_down, mix_norm, w_in,
              qkv_conv_w, a_log, dt_bias, head_norm, w_a_out, dw_conv_w, dw_conv_b,
              conv_norm, w_b_out, w_o, ffn2_norm, ffn2_w_gate, ffn2_w_up, ffn2_w_down,
              final_norm):
    for l in range(DEPTH):
        x = x + 0.5 * swiglu(rmsnorm(x, ffn1_norm[l]), ffn1_w_gate[l], ffn1_w_up[l], ffn1_w_down[l])
        x = x + hybrid_mixer(rmsnorm(x, mix_norm[l]), w_in[l], qkv_conv_w[l], a_log[l], dt_bias[l],
                             head_norm[l], w_a_out[l], dw_conv_w[l], dw_conv_b[l], conv_norm[l],
                             w_b_out[l], w_o[l])
        x = x + 0.5 * swiglu(rmsnorm(x, ffn2_norm[l]), ffn2_w_gate[l], ffn2_w_up[l], ffn2_w_down[l])
    return rmsnorm(x, final_norm)
```

```python
import functools

import jax
import jax.numpy as jnp
from jax import lax
from jax.experimental import pallas as pl
from jax.experimental.pallas import tpu as pltpu

D_MODEL = 1024
N_HEADS = 8
HEAD_DIM = 128
KEY_WIDTH = N_HEADS * HEAD_DIM
CHUNK = 64
SHORT_CONV = 4
DW_WIDTH = 31
D_FF = 2816
EPS = 1e-6

LANES = 128
VMEM_LIMIT = 56 * 1024 * 1024

F32 = jnp.float32
BF16 = jnp.bfloat16


def _rms(x, w):
    return x * lax.rsqrt(jnp.mean(x * x, axis=-1, keepdims=True) + EPS) * w


def _sigmoid(x):
    return 1.0 / (1.0 + jnp.exp(-x))


def _silu(x):
    return x * _sigmoid(x)


def _softplus(x):
    return jnp.maximum(x, 0.0) + jnp.log1p(jnp.exp(-jnp.abs(x)))


def _dot(a, b):
    return jnp.dot(a, b, preferred_element_type=F32)


def _dot_nt(a, b):
    return lax.dot_general(a, b, (((1,), (1,)), ((), ())), preferred_element_type=F32)


def _params(sem):
    return pltpu.CompilerParams(dimension_semantics=sem, vmem_limit_bytes=VMEM_LIMIT)


def _resident(shape):
    nd = len(shape)
    return pl.BlockSpec(shape, lambda *_: (0,) * nd, pipeline_mode=pl.Buffered(1))


FFN_TM = 1024
FFN_TF = 256


def _ffn_kernel(x_ref, nw_ref, wg_ref, wu_ref, wd_ref, fw_ref, o_ref, hn_ref, acc_ref, *, final):
    j = pl.program_id(1)

    @pl.when(j == 0)
    def _():
        hn_ref[...] = _rms(x_ref[...], nw_ref[...]).astype(BF16)
        acc_ref[...] = jnp.zeros_like(acc_ref)

    hn = hn_ref[...]
    g = _dot(hn, wg_ref[...])
    u = _dot(hn, wu_ref[...])
    a = (_silu(g) * u).astype(BF16)
    acc_ref[...] += _dot(a, wd_ref[...])

    @pl.when(j == pl.num_programs(1) - 1)
    def _():
        y = x_ref[...] + 0.5 * acc_ref[...]
        if final:
            y = _rms(y, fw_ref[...])
        o_ref[...] = y


def _ffn(x, nw, wg, wu, wd, fw, final):
    n = x.shape[0]
    tm = min(FFN_TM, n)
    grid = (n // tm, D_FF // FFN_TF)
    return pl.pallas_call(
        functools.partial(_ffn_kernel, final=final),
        grid=grid,
        in_specs=[
            pl.BlockSpec((tm, D_MODEL), lambda i, j: (i, 0)),
            pl.BlockSpec((1, D_MODEL), lambda i, j: (0, 0)),
            pl.BlockSpec((D_MODEL, FFN_TF), lambda i, j: (0, j)),
            pl.BlockSpec((D_MODEL, FFN_TF), lambda i, j: (0, j)),
            pl.BlockSpec((FFN_TF, D_MODEL), lambda i, j: (j, 0)),
            pl.BlockSpec((1, D_MODEL), lambda i, j: (0, 0)),
        ],
        out_specs=pl.BlockSpec((tm, D_MODEL), lambda i, j: (i, 0)),
        out_shape=jax.ShapeDtypeStruct((n, D_MODEL), F32),
        scratch_shapes=[pltpu.VMEM((tm, D_MODEL), BF16), pltpu.VMEM((tm, D_MODEL), F32)],
        compiler_params=_params(("parallel", "arbitrary")),
        name="ffn",
    )(x, nw, wg, wu, wd, fw)


INPROJ_TM = 256
INPROJ_CW = 512
_C_QKV = 0
_C_GOUT = 3 * KEY_WIDTH
_C_VAL = _C_GOUT + KEY_WIDTH
_C_GT = _C_VAL + D_MODEL
_C_GATES = _C_GT + D_MODEL
_C_END = _C_GATES + 2 * D_MODEL


def _inproj_kernel(x_ref, nw_ref, w_ref, wab_ref, qkv_ref, sg_ref, ab_ref, c_ref, gates_ref):
    hn = _rms(x_ref[...], nw_ref[...]).astype(BF16)
    cw = INPROJ_CW
    for c0 in range(0, 3 * KEY_WIDTH, cw):
        qkv_ref[:, c0:c0 + cw] = _dot(hn, w_ref[:, _C_QKV + c0:_C_QKV + c0 + cw])
    for c0 in range(0, KEY_WIDTH, cw):
        sg_ref[:, c0:c0 + cw] = _silu(_dot(hn, w_ref[:, _C_GOUT + c0:_C_GOUT + c0 + cw]))
    ab_ref[...] = _dot(hn, wab_ref[...])
    for c0 in range(0, D_MODEL, cw):
        val = _dot(hn, w_ref[:, _C_VAL + c0:_C_VAL + c0 + cw])
        gt = _dot(hn, w_ref[:, _C_GT + c0:_C_GT + c0 + cw])
        c_ref[:, c0:c0 + cw] = val * _sigmoid(gt)
    for c0 in range(0, 2 * D_MODEL, cw):
        gates_ref[:, c0:c0 + cw] = _sigmoid(_dot(hn, w_ref[:, _C_GATES + c0:_C_GATES + c0 + cw]))


def _inproj(x, nw, w_main, w_ab):
    n = x.shape[0]
    tm = min(INPROJ_TM, n)
    row = lambda width: pl.BlockSpec((tm, width), lambda i: (i, 0))
    return pl.pallas_call(
        _inproj_kernel,
        grid=(n // tm,),
        in_specs=[row(D_MODEL), _resident((1, D_MODEL)), _resident((D_MODEL, _C_END)),
                  _resident((D_MODEL, LANES))],
        out_specs=[row(3 * KEY_WIDTH), row(KEY_WIDTH), row(LANES), row(D_MODEL), row(2 * D_MODEL)],
        out_shape=[
            jax.ShapeDtypeStruct((n, 3 * KEY_WIDTH), F32),
            jax.ShapeDtypeStruct((n, KEY_WIDTH), F32),
            jax.ShapeDtypeStruct((n, LANES), F32),
            jax.ShapeDtypeStruct((n, D_MODEL), F32),
            jax.ShapeDtypeStruct((n, 2 * D_MODEL), F32),
        ],
        compiler_params=_params(("parallel",)),
        name="inproj",
    )(x, nw, w_main, w_ab)


DN_TB = 256
DN_HALO = 8


def _split3(x):
    hi = x.astype(BF16)
    r1 = x - hi.astype(F32)
    mid = r1.astype(BF16)
    lo = (r1 - mid.astype(F32)).astype(BF16)
    return hi, mid, lo


def _column(arr, lane):
    ids = lax.broadcasted_iota(jnp.int32, arr.shape, 1)
    return jnp.sum(jnp.where(ids == lane, arr, 0.0), axis=-1, keepdims=True)


def _deltanet_kernel(qkv_ref, sg_ref, ab_ref, cw_ref, alog_ref, dtb_ref, hn_ref, o_ref,
                     xbuf, ybuf, gc_s, beta_s, gct_s, state_s):
    tb = qkv_ref.shape[0]
    nch = tb // CHUNK
    t = pl.program_id(1)

    @pl.when(t == 0)
    def _():
        xbuf[0:DN_HALO, :] = jnp.zeros((DN_HALO, 3 * KEY_WIDTH), F32)
        state_s[...] = jnp.zeros_like(state_s)

    xbuf[DN_HALO:DN_HALO + tb, :] = qkv_ref[...]
    base = DN_HALO - (SHORT_CONV - 1)
    for c0 in range(0, 3 * KEY_WIDTH, 512):
        acc = cw_ref[0:1, c0:c0 + 512] * xbuf[base:base + tb, c0:c0 + 512]
        for i in range(1, SHORT_CONV):
            acc = acc + cw_ref[i:i + 1, c0:c0 + 512] * xbuf[base + i:base + i + tb, c0:c0 + 512]
        ybuf[:, c0:c0 + 512] = _silu(acc)
    xbuf[0:DN_HALO, :] = xbuf[tb:tb + DN_HALO, :]

    ab = ab_ref[...]
    g = -jnp.exp(alog_ref[...]) * _softplus(ab + dtb_ref[...])
    beta_s[...] = _sigmoid(ab)
    r = lax.broadcasted_iota(jnp.int32, (tb, tb), 0)
    c = lax.broadcasted_iota(jnp.int32, (tb, tb), 1)
    tri = jnp.where((r // CHUNK == c // CHUNK) & (c <= r), 1.0, 0.0).astype(BF16)
    g_hi, g_mid, g_lo = _split3(g)
    gc = _dot(tri, g_hi) + _dot(tri, g_mid) + _dot(tri, g_lo)
    gc_s[...] = gc
    gct = gc.T
    for p in range(nch // 2):
        seg = gct[0:N_HEADS, p * LANES:(p + 1) * LANES]
        gct_s[2 * p] = seg
        gct_s[2 * p + 1] = pltpu.roll(seg, CHUNK, axis=1)

    ri = lax.broadcasted_iota(jnp.int32, (CHUNK, CHUNK), 0)
    ci = lax.broadcasted_iota(jnp.int32, (CHUNK, CHUNK), 1)
    incl = ri >= ci
    strict = ri > ci
    eye = jnp.where(ri == ci, 1.0, 0.0)
    hnw = hn_ref[...]

    def chunk_body(ch, carry):
        rows = pl.ds(pl.multiple_of(ch * CHUNK, CHUNK), CHUNK)
        gc_blk = gc_s[rows, :]
        beta_blk = beta_s[rows, :]
        for h in range(N_HEADS):
            lanes = slice(h * HEAD_DIM, (h + 1) * HEAD_DIM)
            q = ybuf[rows, h * HEAD_DIM:(h + 1) * HEAD_DIM]
            k = ybuf[rows, KEY_WIDTH + h * HEAD_DIM:KEY_WIDTH + (h + 1) * HEAD_DIM]
            v = ybuf[rows, 2 * KEY_WIDTH + h * HEAD_DIM:2 * KEY_WIDTH + (h + 1) * HEAD_DIM]
            q = q * lax.rsqrt(jnp.sum(q * q, axis=-1, keepdims=True) + EPS) * (HEAD_DIM ** -0.5)
            k = k * lax.rsqrt(jnp.sum(k * k, axis=-1, keepdims=True) + EPS)
            beta = _column(beta_blk, N_HEADS + h)
            gcc = _column(gc_blk, h)
            gcr = gct_s[ch, pl.ds(h, 1), :][:, 0:CHUNK]
            g_last = gcc[CHUNK - 1:CHUNK, :]
            e_incl = jnp.exp(jnp.where(incl, gcc - gcr, -jnp.inf))
            e_strict = jnp.where(strict, e_incl, 0.0)
            egc = jnp.exp(gcc)

            k16 = k.astype(BF16)
            kb = k * beta
            m = _dot_nt(kb.astype(BF16), k16) * e_strict
            attn = _dot_nt(q.astype(BF16), k16) * e_incl

            p = -m
            tinv = eye + p
            for _ in range(5):
                p16 = p.astype(BF16)
                p = _dot(p16, p16)
                tinv = tinv + _dot(tinv.astype(BF16), p.astype(BF16))

            rhs = jnp.concatenate([v * beta, kb * egc], axis=-1).astype(BF16)
            sol = _dot(tinv.astype(BF16), rhs)
            u = sol[:, 0:HEAD_DIM]
            w = sol[:, HEAD_DIM:2 * HEAD_DIM]

            s = state_s[h]
            s16 = s.astype(BF16)
            v_new = u - _dot(w.astype(BF16), s16)
            v_new16 = v_new.astype(BF16)
            o = _dot((q * egc).astype(BF16), s16) + _dot(attn.astype(BF16), v_new16)
            kd = k * jnp.exp(g_last - gcc)
            state_s[h] = s * jnp.exp(g_last) + _dot(kd.T.astype(BF16), v_new16)

            o = _rms(o, hnw) * sg_ref[rows, lanes]
            o_ref[rows, lanes] = o.astype(o_ref.dtype)
        return carry

    lax.fori_loop(0, nch, chunk_body, 0)


def _deltanet(qkv, sg, ab, conv_w, alog_pad, dtb_pad, head_norm, bsz, seq):
    n = qkv.shape[0]
    tb = min(DN_TB, seq)
    nt = seq // tb
    nch = tb // CHUNK
    row = lambda width: pl.BlockSpec((tb, width), lambda b, t: (b * nt + t, 0))
    return pl.pallas_call(
        _deltanet_kernel,
        grid=(bsz, nt),
        in_specs=[row(3 * KEY_WIDTH), row(KEY_WIDTH), row(LANES),
                  pl.BlockSpec((SHORT_CONV, 3 * KEY_WIDTH), lambda b, t: (0, 0)),
                  pl.BlockSpec((1, LANES), lambda b, t: (0, 0)),
                  pl.BlockSpec((1, LANES), lambda b, t: (0, 0)),
                  pl.BlockSpec((1, HEAD_DIM), lambda b, t: (0, 0))],
        out_specs=row(KEY_WIDTH),
        out_shape=jax.ShapeDtypeStruct((n, KEY_WIDTH), BF16),
        scratch_shapes=[
            pltpu.VMEM((tb + DN_HALO, 3 * KEY_WIDTH), F32),
            pltpu.VMEM((tb, 3 * KEY_WIDTH), F32),
            pltpu.VMEM((tb, LANES), F32),
            pltpu.VMEM((tb, LANES), F32),
            pltpu.VMEM((nch, N_HEADS, LANES), F32),
            pltpu.VMEM((N_HEADS, HEAD_DIM, HEAD_DIM), F32),
        ],
        compiler_params=_params(("parallel", "arbitrary")),
        name="deltanet",
    )(qkv, sg, ab, conv_w, alog_pad, dtb_pad, head_norm)


CM_TB = 256
CM_HALO = 32
CM_RB = 64
CM_LB = 256


def _convmod_kernel(c_ref, w_ref, b_ref, nw_ref, o_ref, xbuf, ybuf):
    tb = c_ref.shape[0]
    t = pl.program_id(1)

    @pl.when(t == 0)
    def _():
        xbuf[0:CM_HALO, :] = jnp.zeros((CM_HALO, D_MODEL), F32)

    xbuf[CM_HALO:CM_HALO + tb, :] = c_ref[...]
    base = CM_HALO - (DW_WIDTH - 1)
    for r0 in range(0, tb, CM_RB):
        for c0 in range(0, D_MODEL, CM_LB):
            acc = jnp.zeros((CM_RB, CM_LB), F32) + b_ref[:, c0:c0 + CM_LB]
            for i in range(DW_WIDTH):
                acc = acc + (w_ref[i:i + 1, c0:c0 + CM_LB]
                             * xbuf[base + r0 + i:base + r0 + i + CM_RB, c0:c0 + CM_LB])
            ybuf[r0:r0 + CM_RB, c0:c0 + CM_LB] = acc
    xbuf[0:CM_HALO, :] = xbuf[tb:tb + CM_HALO, :]
    o_ref[...] = _silu(_rms(ybuf[...], nw_ref[...])).astype(o_ref.dtype)


def _convmod(c, w, b, nw, bsz, seq):
    n = c.shape[0]
    tb = min(CM_TB, seq)
    nt = seq // tb
    row = pl.BlockSpec((tb, D_MODEL), lambda bi, t: (bi * nt + t, 0))
    return pl.pallas_call(
        _convmod_kernel,
        grid=(bsz, nt),
        in_specs=[row,
                  pl.BlockSpec((DW_WIDTH, D_MODEL), lambda bi, t: (0, 0)),
                  pl.BlockSpec((1, D_MODEL), lambda bi, t: (0, 0)),
                  pl.BlockSpec((1, D_MODEL), lambda bi, t: (0, 0))],
        out_specs=row,
        out_shape=jax.ShapeDtypeStruct((n, D_MODEL), BF16),
        scratch_shapes=[pltpu.VMEM((tb + CM_HALO, D_MODEL), F32), pltpu.VMEM((tb, D_MODEL), F32)],
        compiler_params=_params(("parallel", "arbitrary")),
        name="convmod",
    )(c, w, b, nw)


OUT_TM = 512


def _outproj_kernel(x_ref, oa_ref, cb_ref, gates_ref, wa_ref, wb_ref, wo_ref, o_ref):
    ya = _dot(oa_ref[...], wa_ref[...])
    yb = _dot(cb_ref[...], wb_ref[...])
    merged = gates_ref[:, 0:D_MODEL] * ya + gates_ref[:, D_MODEL:2 * D_MODEL] * yb
    o_ref[...] = x_ref[...] + _dot(merged.astype(BF16), wo_ref[...])


def _outproj(x, oa, cb, gates, wa, wb, wo):
    n = x.shape[0]
    tm = min(OUT_TM, n)
    row = lambda width: pl.BlockSpec((tm, width), lambda i: (i, 0))
    wspec = pl.BlockSpec((D_MODEL, D_MODEL), lambda i: (0, 0))
    return pl.pallas_call(
        _outproj_kernel,
        grid=(n // tm,),
        in_specs=[row(D_MODEL), row(KEY_WIDTH), row(D_MODEL), row(2 * D_MODEL), wspec, wspec, wspec],
        out_specs=row(D_MODEL),
        out_shape=jax.ShapeDtypeStruct((n, D_MODEL), F32),
        compiler_params=_params(("parallel",)),
        name="outproj",
    )(x, oa, cb, gates, wa, wb, wo)


def _pad_lanes(v, offset=0):
    return jnp.zeros((1, LANES), F32).at[0, offset:offset + v.shape[0]].set(v.astype(F32))


def kernel(x, ffn1_norm, ffn1_w_gate, ffn1_w_up, ffn1_w_down, mix_norm, w_in, qkv_conv_w, a_log, dt_bias, head_norm, w_a_out, dw_conv_w, dw_conv_b, conv_norm, w_b_out, w_o, ffn2_norm, ffn2_w_gate, ffn2_w_up, ffn2_w_down, final_norm):
    bsz, seq, _ = x.shape
    depth = w_in.shape[0]
    h = x.reshape(bsz * seq, D_MODEL)
    fw = final_norm.reshape(1, D_MODEL)
    n_ab = 2 * N_HEADS
    c_ab = 4 * KEY_WIDTH
    for l in range(depth):
        h = _ffn(h, ffn1_norm[l].reshape(1, -1), ffn1_w_gate[l].astype(BF16), ffn1_w_up[l].astype(BF16),
                 ffn1_w_down[l].astype(BF16), fw, final=False)
        w_main = jnp.concatenate([w_in[l][:, :c_ab], w_in[l][:, c_ab + n_ab:]], axis=1).astype(BF16)
        w_ab = jnp.pad(w_in[l][:, c_ab:c_ab + n_ab], ((0, 0), (0, LANES - n_ab))).astype(BF16)
        qkv, sg, ab, c, gates = _inproj(h, mix_norm[l].reshape(1, -1), w_main, w_ab)
        oa = _deltanet(qkv, sg, ab, qkv_conv_w[l], _pad_lanes(a_log[l]), _pad_lanes(dt_bias[l]),
                       head_norm[l].reshape(1, -1), bsz, seq)
        cb = _convmod(c, dw_conv_w[l], dw_conv_b[l].reshape(1, -1), conv_norm[l].reshape(1, -1), bsz, seq)
        h = _outproj(h, oa, cb, gates, w_a_out[l].astype(BF16), w_b_out[l].astype(BF16), w_o[l].astype(BF16))
        h = _ffn(h, ffn2_norm[l].reshape(1, -1), ffn2_w_gate[l].astype(BF16), ffn2_w_up[l].astype(BF16),
                 ffn2_w_down[l].astype(BF16), fw, final=(l == depth - 1))
    return h.reshape(bsz, seq, D_MODEL)
```

```python
import functools

import jax
import jax.numpy as jnp
from jax import lax
from jax.experimental import pallas as pl
from jax.experimental.pallas import tpu as pltpu

D_MODEL = 1024
N_HEADS = 8
HEAD_DIM = 128
KEY_WIDTH = N_HEADS * HEAD_DIM
CHUNK = 64
SHORT_CONV = 4
DW_WIDTH = 31
D_FF = 2816
EPS = 1e-6

LANES = 128
VMEM_LIMIT = 56 * 1024 * 1024

F32 = jnp.float32
BF16 = jnp.bfloat16


def _rms(x, w):
    return x * lax.rsqrt(jnp.mean(x * x, axis=-1, keepdims=True) + EPS) * w


def _sigmoid(x):
    return 1.0 / (1.0 + jnp.exp(-x))


def _silu(x):
    return x * _sigmoid(x)


def _softplus(x):
    return jnp.maximum(x, 0.0) + jnp.log1p(jnp.exp(-jnp.abs(x)))


def _dot(a, b):
    return jnp.dot(a, b, preferred_element_type=F32)


def _dot_nt(a, b):
    return lax.dot_general(a, b, (((1,), (1,)), ((), ())), preferred_element_type=F32)


def _params(sem):
    return pltpu.CompilerParams(dimension_semantics=sem, vmem_limit_bytes=VMEM_LIMIT)


def _resident(shape):
    nd = len(shape)
    return pl.BlockSpec(shape, lambda *_: (0,) * nd, pipeline_mode=pl.Buffered(1))


FFN_TM = 1024
FFN_TF = 256


def _ffn_kernel(x_ref, nw_ref, wg_ref, wu_ref, wd_ref, fw_ref, o_ref, hn_ref, acc_ref, *, final):
    j = pl.program_id(1)

    @pl.when(j == 0)
    def _():
        hn_ref[...] = _rms(x_ref[...], nw_ref[...]).astype(BF16)
        acc_ref[...] = jnp.zeros_like(acc_ref)

    hn = hn_ref[...]
    g = _dot(hn, wg_ref[...])
    u = _dot(hn, wu_ref[...])
    a = (_silu(g) * u).astype(BF16)
    acc_ref[...] += _dot(a, wd_ref[...])

    @pl.when(j == pl.num_programs(1) - 1)
    def _():
        y = x_ref[...] + 0.5 * acc_ref[...]
        if final:
            y = _rms(y, fw_ref[...])
        o_ref[...] = y


def _ffn(x, nw, wg, wu, wd, fw, final):
    n = x.shape[0]
    tm = min(FFN_TM, n)
    grid = (n // tm, D_FF // FFN_TF)
    return pl.pallas_call(
        functools.partial(_ffn_kernel, final=final),
        grid=grid,
        in_specs=[
            pl.BlockSpec((tm, D_MODEL), lambda i, j: (i, 0)),
            pl.BlockSpec((1, D_MODEL), lambda i, j: (0, 0)),
            pl.BlockSpec((D_MODEL, FFN_TF), lambda i, j: (0, j)),
            pl.BlockSpec((D_MODEL, FFN_TF), lambda i, j: (0, j)),
            pl.BlockSpec((FFN_TF, D_MODEL), lambda i, j: (j, 0)),
            pl.BlockSpec((1, D_MODEL), lambda i, j: (0, 0)),
        ],
        out_specs=pl.BlockSpec((tm, D_MODEL), lambda i, j: (i, 0)),
        out_shape=jax.ShapeDtypeStruct((n, D_MODEL), F32),
        scratch_shapes=[pltpu.VMEM((tm, D_MODEL), BF16), pltpu.VMEM((tm, D_MODEL), F32)],
        compiler_params=_params(("parallel", "arbitrary")),
        name="ffn",
    )(x, nw, wg, wu, wd, fw)


INPROJ_TM = 256
INPROJ_CW = 512
_C_QKV = 0
_C_GOUT = 3 * KEY_WIDTH
_C_VAL = _C_GOUT + KEY_WIDTH
_C_GT = _C_VAL + D_MODEL
_C_GATES = _C_GT + D_MODEL
_C_END = _C_GATES + 2 * D_MODEL


def _inproj_kernel(x_ref, nw_ref, w_ref, wab_ref, qkv_ref, sg_ref, ab_ref, c_ref, gates_ref):
    hn = _rms(x_ref[...], nw_ref[...]).astype(BF16)
    cw = INPROJ_CW
    for c0 in range(0, 3 * KEY_WIDTH, cw):
        qkv_ref[:, c0:c0 + cw] = _dot(hn, w_ref[:, _C_QKV + c0:_C_QKV + c0 + cw])
    for c0 in range(0, KEY_WIDTH, cw):
        sg_ref[:, c0:c0 + cw] = _silu(_dot(hn, w_ref[:, _C_GOUT + c0:_C_GOUT + c0 + cw]))
    ab_ref[...] = _dot(hn, wab_ref[...])
    for c0 in range(0, D_MODEL, cw):
        val = _dot(hn, w_ref[:, _C_VAL + c0:_C_VAL + c0 + cw])
        gt = _dot(hn, w_ref[:, _C_GT + c0:_C_GT + c0 + cw])
        c_ref[:, c0:c0 + cw] = val * _sigmoid(gt)
    for c0 in range(0, 2 * D_MODEL, cw):
        gates_ref[:, c0:c0 + cw] = _sigmoid(_dot(hn, w_ref[:, _C_GATES + c0:_C_GATES + c0 + cw]))


def _inproj(x, nw, w_main, w_ab):
    n = x.shape[0]
    tm = min(INPROJ_TM, n)
    row = lambda width: pl.BlockSpec((tm, width), lambda i: (i, 0))
    return pl.pallas_call(
        _inproj_kernel,
        grid=(n // tm,),
        in_specs=[row(D_MODEL), _resident((1, D_MODEL)), _resident((D_MODEL, _C_END)),
                  _resident((D_MODEL, LANES))],
        out_specs=[row(3 * KEY_WIDTH), row(KEY_WIDTH), row(LANES), row(D_MODEL), row(2 * D_MODEL)],
        out_shape=[
            jax.ShapeDtypeStruct((n, 3 * KEY_WIDTH), F32),
            jax.ShapeDtypeStruct((n, KEY_WIDTH), F32),
            jax.ShapeDtypeStruct((n, LANES), F32),
            jax.ShapeDtypeStruct((n, D_MODEL), F32),
            jax.ShapeDtypeStruct((n, 2 * D_MODEL), F32),
        ],
        compiler_params=_params(("parallel",)),
        name="inproj",
    )(x, nw, w_main, w_ab)


DN_TB = 256
DN_HALO = 8
DN_PA = 2


def _split3(x):
    hi = x.astype(BF16)
    r1 = x - hi.astype(F32)
    mid = r1.astype(BF16)
    lo = (r1 - mid.astype(F32)).astype(BF16)
    return hi, mid, lo


def _column(arr, lane):
    ids = lax.broadcasted_iota(jnp.int32, arr.shape, 1)
    return jnp.sum(jnp.where(ids == lane, arr, 0.0), axis=-1, keepdims=True)


def _deltanet_kernel(qkv_ref, sg_ref, ab_ref, cw_ref, alog_ref, dtb_ref, hn_ref, o_ref,
                     xbuf, ybuf, gc_s, beta_s, gct_s, state_s, lhs_s, bm_s, ou_s, eg_s):
    tb = qkv_ref.shape[0]
    nch = tb // CHUNK
    t = pl.program_id(1)

    @pl.when(t == 0)
    def _():
        xbuf[0:DN_HALO, :] = jnp.zeros((DN_HALO, 3 * KEY_WIDTH), F32)
        state_s[...] = jnp.zeros_like(state_s)

    xbuf[DN_HALO:DN_HALO + tb, :] = qkv_ref[...]
    base = DN_HALO - (SHORT_CONV - 1)
    for c0 in range(0, 3 * KEY_WIDTH, 512):
        acc = cw_ref[0:1, c0:c0 + 512] * xbuf[base:base + tb, c0:c0 + 512]
        for i in range(1, SHORT_CONV):
            acc = acc + cw_ref[i:i + 1, c0:c0 + 512] * xbuf[base + i:base + i + tb, c0:c0 + 512]
        ybuf[:, c0:c0 + 512] = _silu(acc)
    xbuf[0:DN_HALO, :] = xbuf[tb:tb + DN_HALO, :]

    ab = ab_ref[...]
    g = -jnp.exp(alog_ref[...]) * _softplus(ab + dtb_ref[...])
    beta_s[...] = _sigmoid(ab)
    r = lax.broadcasted_iota(jnp.int32, (tb, tb), 0)
    c = lax.broadcasted_iota(jnp.int32, (tb, tb), 1)
    tri = jnp.where((r // CHUNK == c // CHUNK) & (c <= r), 1.0, 0.0).astype(BF16)
    g_hi, g_mid, g_lo = _split3(g)
    gc = _dot(tri, g_hi) + _dot(tri, g_mid) + _dot(tri, g_lo)
    gc_s[...] = gc
    gct = gc.T
    for p in range(nch // 2):
        seg = gct[0:N_HEADS, p * LANES:(p + 1) * LANES]
        gct_s[2 * p] = seg
        gct_s[2 * p + 1] = pltpu.roll(seg, CHUNK, axis=1)

    ri = lax.broadcasted_iota(jnp.int32, (CHUNK, CHUNK), 0)
    ci = lax.broadcasted_iota(jnp.int32, (CHUNK, CHUNK), 1)
    incl = ri >= ci
    strict = ri > ci
    eye = jnp.where(ri == ci, 1.0, 0.0)
    hnw = hn_ref[...]

    def head_lanes(h, group=0):
        return slice(group * KEY_WIDTH + h * HEAD_DIM, group * KEY_WIDTH + (h + 1) * HEAD_DIM)

    def phase_a(it, carry):
        probs = [(it * DN_PA + cc, h) for cc in range(DN_PA) for h in range(N_HEADS)]
        rows_of = {}
        for cc in range(DN_PA):
            rows = pl.ds(pl.multiple_of((it * DN_PA + cc) * CHUNK, CHUNK), CHUNK)
            rows_of[cc] = (rows, gc_s[rows, :], beta_s[rows, :])
        pre = []
        for i, (ch, h) in enumerate(probs):
            rows, gc_blk, beta_blk = rows_of[i // N_HEADS]
            q = ybuf[rows, head_lanes(h, 0)]
            k = ybuf[rows, head_lanes(h, 1)]
            v = ybuf[rows, head_lanes(h, 2)]
            q = q * lax.rsqrt(jnp.sum(q * q, axis=-1, keepdims=True) + EPS) * (HEAD_DIM ** -0.5)
            k = k * lax.rsqrt(jnp.sum(k * k, axis=-1, keepdims=True) + EPS)
            beta = _column(beta_blk, N_HEADS + h)
            gcc = _column(gc_blk, h)
            gcr = gct_s[ch, pl.ds(h, 1), :][:, 0:CHUNK]
            g_last = gcc[CHUNK - 1:CHUNK, :]
            e_incl = jnp.exp(jnp.where(incl, gcc - gcr, -jnp.inf))
            egc = jnp.exp(gcc)
            kb = k * beta
            eg_s[ch * N_HEADS + h] = jnp.broadcast_to(jnp.exp(g_last), (8, LANES))
            pre.append(dict(
                q16=q.astype(BF16), k16=k.astype(BF16), kb16=kb.astype(BF16), e_incl=e_incl,
                qd=q * egc, rhs=jnp.concatenate([v * beta, kb * egc], axis=-1).astype(BF16),
                kdt16=(k * jnp.exp(g_last - gcc)).T.astype(BF16)))

        ms = [_dot_nt(d["kb16"], d["k16"]) * jnp.where(strict, d["e_incl"], 0.0) for d in pre]
        attns = [(_dot_nt(d["q16"], d["k16"]) * d["e_incl"]).astype(BF16) for d in pre]

        ps = [-m for m in ms]
        tinvs = [eye + p for p in ps]
        for _ in range(5):
            p16s = [p.astype(BF16) for p in ps]
            ps = [_dot(p16, p16) for p16 in p16s]
            tinvs = [tv + _dot(tv.astype(BF16), p.astype(BF16)) for tv, p in zip(tinvs, ps)]

        uws = [_dot(tv.astype(BF16), d["rhs"]).astype(BF16) for tv, d in zip(tinvs, pre)]
        kuws = [_dot(d["kdt16"], uw) for d, uw in zip(pre, uws)]
        auws = [_dot(a, uw) for a, uw in zip(attns, uws)]
        for (ch, h), d, kuw, auw in zip(probs, pre, kuws, auws):
            idx = ch * N_HEADS + h
            rows = pl.ds(pl.multiple_of(ch * CHUNK, CHUNK), CHUNK)
            qeff = d["qd"] - auw[:, HEAD_DIM:2 * HEAD_DIM]
            lhs_s[idx, 0:CHUNK, :] = qeff.astype(BF16)
            lhs_s[idx, CHUNK:CHUNK + HEAD_DIM, :] = kuw[:, HEAD_DIM:2 * HEAD_DIM].astype(BF16)
            bm_s[idx] = kuw[:, 0:HEAD_DIM]
            ou_s[rows, head_lanes(h)] = auw[:, 0:HEAD_DIM]
        return carry

    lax.fori_loop(0, nch // DN_PA, phase_a, 0)

    def phase_b(ch, carry):
        rows = pl.ds(pl.multiple_of(ch * CHUNK, CHUNK), CHUNK)
        res = [_dot(lhs_s[ch * N_HEADS + h], state_s[h].astype(BF16)) for h in range(N_HEADS)]
        for h in range(N_HEADS):
            idx = ch * N_HEADS + h
            state_s[h] = (state_s[h] * eg_s[idx][0:1, :] - res[h][CHUNK:CHUNK + HEAD_DIM, :] + bm_s[idx])
            o = res[h][0:CHUNK, :] + ou_s[rows, head_lanes(h)]
            o = _rms(o, hnw) * sg_ref[rows, head_lanes(h)]
            o_ref[rows, head_lanes(h)] = o.astype(o_ref.dtype)
        return carry

    lax.fori_loop(0, nch, phase_b, 0)


def _deltanet(qkv, sg, ab, conv_w, alog_pad, dtb_pad, head_norm, bsz, seq):
    n = qkv.shape[0]
    tb = min(DN_TB, seq)
    nt = seq // tb
    nch = tb // CHUNK
    row = lambda width: pl.BlockSpec((tb, width), lambda b, t: (b * nt + t, 0))
    return pl.pallas_call(
        _deltanet_kernel,
        grid=(bsz, nt),
        in_specs=[row(3 * KEY_WIDTH), row(KEY_WIDTH), row(LANES),
                  pl.BlockSpec((SHORT_CONV, 3 * KEY_WIDTH), lambda b, t: (0, 0)),
                  pl.BlockSpec((1, LANES), lambda b, t: (0, 0)),
                  pl.BlockSpec((1, LANES), lambda b, t: (0, 0)),
                  pl.BlockSpec((1, HEAD_DIM), lambda b, t: (0, 0))],
        out_specs=row(KEY_WIDTH),
        out_shape=jax.ShapeDtypeStruct((n, KEY_WIDTH), BF16),
        scratch_shapes=[
            pltpu.VMEM((tb + DN_HALO, 3 * KEY_WIDTH), F32),
            pltpu.VMEM((tb, 3 * KEY_WIDTH), F32),
            pltpu.VMEM((tb, LANES), F32),
            pltpu.VMEM((tb, LANES), F32),
            pltpu.VMEM((nch, N_HEADS, LANES), F32),
            pltpu.VMEM((N_HEADS, HEAD_DIM, HEAD_DIM), F32),
            pltpu.VMEM((nch * N_HEADS, CHUNK + HEAD_DIM, HEAD_DIM), BF16),
            pltpu.VMEM((nch * N_HEADS, HEAD_DIM, HEAD_DIM), F32),
            pltpu.VMEM((tb, KEY_WIDTH), F32),
            pltpu.VMEM((nch * N_HEADS, 8, LANES), F32),
        ],
        compiler_params=_params(("parallel", "arbitrary")),
        name="deltanet",
    )(qkv, sg, ab, conv_w, alog_pad, dtb_pad, head_norm)


CM_TB = 256
CM_HALO = 32
CM_RB = 64
CM_LB = 256


def _convmod_kernel(c_ref, w_ref, b_ref, nw_ref, o_ref, xbuf, ybuf):
    tb = c_ref.shape[0]
    t = pl.program_id(1)

    @pl.when(t == 0)
    def _():
        xbuf[0:CM_HALO, :] = jnp.zeros((CM_HALO, D_MODEL), F32)

    xbuf[CM_HALO:CM_HALO + tb, :] = c_ref[...]
    base = CM_HALO - (DW_WIDTH - 1)
    for r0 in range(0, tb, CM_RB):
        for c0 in range(0, D_MODEL, CM_LB):
            acc = jnp.zeros((CM_RB, CM_LB), F32) + b_ref[:, c0:c0 + CM_LB]
            for i in range(DW_WIDTH):
                acc = acc + (w_ref[i:i + 1, c0:c0 + CM_LB]
                             * xbuf[base + r0 + i:base + r0 + i + CM_RB, c0:c0 + CM_LB])
            ybuf[r0:r0 + CM_RB, c0:c0 + CM_LB] = acc
    xbuf[0:CM_HALO, :] = xbuf[tb:tb + CM_HALO, :]
    o_ref[...] = _silu(_rms(ybuf[...], nw_ref[...])).astype(o_ref.dtype)


def _convmod(c, w, b, nw, bsz, seq):
    n = c.shape[0]
    tb = min(CM_TB, seq)
    nt = seq // tb
    row = pl.BlockSpec((tb, D_MODEL), lambda bi, t: (bi * nt + t, 0))
    return pl.pallas_call(
        _convmod_kernel,
        grid=(bsz, nt),
        in_specs=[row,
                  pl.BlockSpec((DW_WIDTH, D_MODEL), lambda bi, t: (0, 0)),
                  pl.BlockSpec((1, D_MODEL), lambda bi, t: (0, 0)),
                  pl.BlockSpec((1, D_MODEL), lambda bi, t: (0, 0))],
        out_specs=row,
        out_shape=jax.ShapeDtypeStruct((n, D_MODEL), BF16),
        scratch_shapes=[pltpu.VMEM((tb + CM_HALO, D_MODEL), F32), pltpu.VMEM((tb, D_MODEL), F32)],
        compiler_params=_params(("parallel", "arbitrary")),
        name="convmod",
    )(c, w, b, nw)


OUT_TM = 512


def _outproj_kernel(x_ref, oa_ref, cb_ref, gates_ref, wa_ref, wb_ref, wo_ref, o_ref):
    ya = _dot(oa_ref[...], wa_ref[...])
    yb = _dot(cb_ref[...], wb_ref[...])
    merged = gates_ref[:, 0:D_MODEL] * ya + gates_ref[:, D_MODEL:2 * D_MODEL] * yb
    o_ref[...] = x_ref[...] + _dot(merged.astype(BF16), wo_ref[...])


def _outproj(x, oa, cb, gates, wa, wb, wo):
    n = x.shape[0]
    tm = min(OUT_TM, n)
    row = lambda width: pl.BlockSpec((tm, width), lambda i: (i, 0))
    wspec = pl.BlockSpec((D_MODEL, D_MODEL), lambda i: (0, 0))
    return pl.pallas_call(
        _outproj_kernel,
        grid=(n // tm,),
        in_specs=[row(D_MODEL), row(KEY_WIDTH), row(D_MODEL), row(2 * D_MODEL), wspec, wspec, wspec],
        out_specs=row(D_MODEL),
        out_shape=jax.ShapeDtypeStruct((n, D_MODEL), F32),
        compiler_params=_params(("parallel",)),
        name="outproj",
    )(x, oa, cb, gates, wa, wb, wo)


def _pad_lanes(v, offset=0):
    return jnp.zeros((1, LANES), F32).at[0, offset:offset + v.shape[0]].set(v.astype(F32))


def kernel(x, ffn1_norm, ffn1_w_gate, ffn1_w_up, ffn1_w_down, mix_norm, w_in, qkv_conv_w, a_log, dt_bias, head_norm, w_a_out, dw_conv_w, dw_conv_b, conv_norm, w_b_out, w_o, ffn2_norm, ffn2_w_gate, ffn2_w_up, ffn2_w_down, final_norm):
    bsz, seq, _ = x.shape
    depth = w_in.shape[0]
    h = x.reshape(bsz * seq, D_MODEL)
    fw = final_norm.reshape(1, D_MODEL)
    n_ab = 2 * N_HEADS
    c_ab = 4 * KEY_WIDTH
    for l in range(depth):
        h = _ffn(h, ffn1_norm[l].reshape(1, -1), ffn1_w_gate[l].astype(BF16), ffn1_w_up[l].astype(BF16),
                 ffn1_w_down[l].astype(BF16), fw, final=False)
        w_main = jnp.concatenate([w_in[l][:, :c_ab], w_in[l][:, c_ab + n_ab:]], axis=1).astype(BF16)
        w_ab = jnp.pad(w_in[l][:, c_ab:c_ab + n_ab], ((0, 0), (0, LANES - n_ab))).astype(BF16)
        qkv, sg, ab, c, gates = _inproj(h, mix_norm[l].reshape(1, -1), w_main, w_ab)
        oa = _deltanet(qkv, sg, ab, qkv_conv_w[l], _pad_lanes(a_log[l]), _pad_lanes(dt_bias[l]),
                       head_norm[l].reshape(1, -1), bsz, seq)
        cb = _convmod(c, dw_conv_w[l], dw_conv_b[l].reshape(1, -1), conv_norm[l].reshape(1, -1), bsz, seq)
        h = _outproj(h, oa, cb, gates, w_a_out[l].astype(BF16), w_b_out[l].astype(BF16), w_o[l].astype(BF16))
        h = _ffn(h, ffn2_norm[l].reshape(1, -1), ffn2_w_gate[l].astype(BF16), ffn2_w_up[l].astype(BF16),
                 ffn2_w_down[l].astype(BF16), fw, final=(l == depth - 1))
    return h.reshape(bsz, seq, D_MODEL)
```

```python
import functools

import jax
import jax.numpy as jnp
from jax import lax
from jax.experimental import pallas as pl
from jax.experimental.pallas import tpu as pltpu

D_MODEL = 1024
N_HEADS = 8
HEAD_DIM = 128
KEY_WIDTH = N_HEADS * HEAD_DIM
CHUNK = 64
SHORT_CONV = 4
DW_WIDTH = 31
D_FF = 2816
EPS = 1e-6

LANES = 128
SUBLANES = 8
VMEM_LIMIT = 56 * 1024 * 1024

F32 = jnp.float32
BF16 = jnp.bfloat16


def _rms(x, w):
    return x * lax.rsqrt(jnp.mean(x * x, axis=-1, keepdims=True) + EPS) * w


def _sigmoid(x):
    return 1.0 / (1.0 + jnp.exp(-x))


def _silu(x):
    return x * _sigmoid(x)


def _softplus(x):
    return jnp.maximum(x, 0.0) + jnp.log1p(jnp.exp(-jnp.abs(x)))


def _dot(a, b):
    return jnp.dot(a, b, preferred_element_type=F32)


def _dot_nt(a, b):
    return lax.dot_general(a, b, (((1,), (1,)), ((), ())), preferred_element_type=F32)


def _params(sem):
    return pltpu.CompilerParams(dimension_semantics=sem, vmem_limit_bytes=VMEM_LIMIT)


def _resident(shape):
    nd = len(shape)
    return pl.BlockSpec(shape, lambda *_: (0,) * nd, pipeline_mode=pl.Buffered(1))


FFN_TM = 1024
FFN_TF = 256


def _ffn_kernel(x_ref, nw_ref, wg_ref, wu_ref, wd_ref, fw_ref, o_ref, hn_ref, acc_ref, *, final):
    j = pl.program_id(1)

    @pl.when(j == 0)
    def _():
        hn_ref[...] = _rms(x_ref[...], nw_ref[...]).astype(BF16)
        acc_ref[...] = jnp.zeros_like(acc_ref)

    hn = hn_ref[...]
    g = _dot(hn, wg_ref[...])
    u = _dot(hn, wu_ref[...])
    a = (_silu(g) * u).astype(BF16)
    acc_ref[...] += _dot(a, wd_ref[...])

    @pl.when(j == pl.num_programs(1) - 1)
    def _():
        y = x_ref[...] + 0.5 * acc_ref[...]
        if final:
            y = _rms(y, fw_ref[...])
        o_ref[...] = y


def _ffn(x, nw, wg, wu, wd, fw, final):
    n = x.shape[0]
    tm = min(FFN_TM, n)
    grid = (n // tm, D_FF // FFN_TF)
    return pl.pallas_call(
        functools.partial(_ffn_kernel, final=final),
        grid=grid,
        in_specs=[
            pl.BlockSpec((tm, D_MODEL), lambda i, j: (i, 0)),
            pl.BlockSpec((1, D_MODEL), lambda i, j: (0, 0)),
            pl.BlockSpec((D_MODEL, FFN_TF), lambda i, j: (0, j)),
            pl.BlockSpec((D_MODEL, FFN_TF), lambda i, j: (0, j)),
            pl.BlockSpec((FFN_TF, D_MODEL), lambda i, j: (j, 0)),
            pl.BlockSpec((1, D_MODEL), lambda i, j: (0, 0)),
        ],
        out_specs=pl.BlockSpec((tm, D_MODEL), lambda i, j: (i, 0)),
        out_shape=jax.ShapeDtypeStruct((n, D_MODEL), F32),
        scratch_shapes=[pltpu.VMEM((tm, D_MODEL), BF16), pltpu.VMEM((tm, D_MODEL), F32)],
        compiler_params=_params(("parallel", "arbitrary")),
        name="ffn",
    )(x, nw, wg, wu, wd, fw)


INPROJ_TM = 512
INPROJ_GATE_ROWS = 256
INPROJ_CW = 256
INPROJ_RB = 64
_C_GOUT = 3 * KEY_WIDTH
_C_VAL = _C_GOUT + KEY_WIDTH
_C_GT = _C_VAL + D_MODEL
_C_GATES = _C_GT + D_MODEL
_C_END = _C_GATES + 2 * D_MODEL


def _split3(x):
    hi = x.astype(BF16)
    r1 = x - hi.astype(F32)
    mid = r1.astype(BF16)
    lo = (r1 - mid.astype(F32)).astype(BF16)
    return hi, mid, lo


def _inproj_kernel(x_ref, nw_ref, w_ref, wab_ref, cw_ref, alog_ref, dtb_ref,
                   q_ref, k_ref, v_ref, sg_ref, gb_ref, gct_ref, c_ref, gates_ref,
                   hn_s, xb, halo_s, *, tiles_per_seq):
    tm = x_ref.shape[0]
    cw = INPROJ_CW
    rb = INPROJ_RB

    @pl.when(pl.program_id(0) % tiles_per_seq == 0)
    def _():
        halo_s[...] = jnp.zeros_like(halo_s)

    hn_s[...] = _rms(x_ref[...], nw_ref[...]).astype(BF16)

    for group, out_ref in enumerate((q_ref, k_ref, v_ref)):
        g0 = group * KEY_WIDTH
        xc = xb.at[group]
        xc[0:SUBLANES, :] = halo_s[:, g0:g0 + KEY_WIDTH]
        xc[SUBLANES:SUBLANES + tm, :] = _dot(hn_s[...], w_ref[:, g0:g0 + KEY_WIDTH])
        halo_s[:, g0:g0 + KEY_WIDTH] = xc[tm:tm + SUBLANES, :]
    for group, out_ref in enumerate((q_ref, k_ref, v_ref)):
        g0 = group * KEY_WIDTH
        xc = xb.at[group]
        for c0 in range(0, KEY_WIDTH, cw):
            for r0 in range(0, tm, rb):
                xs = xc[r0:r0 + rb + SUBLANES, c0:c0 + cw]
                z = cw_ref[0:1, g0 + c0:g0 + c0 + cw] * xs
                for i in range(1, SHORT_CONV):
                    z = pltpu.roll(z, 1, axis=0) + cw_ref[i:i + 1, g0 + c0:g0 + c0 + cw] * xs
                y = _silu(z[SUBLANES:, :])
                for h0 in range(0, cw, HEAD_DIM):
                    yh = y[:, h0:h0 + HEAD_DIM]
                    if group == 0:
                        yh = yh * lax.rsqrt(jnp.sum(yh * yh, axis=-1, keepdims=True) + EPS) * (HEAD_DIM ** -0.5)
                    elif group == 1:
                        yh = yh * lax.rsqrt(jnp.sum(yh * yh, axis=-1, keepdims=True) + EPS)
                    out_ref[r0:r0 + rb, c0 + h0:c0 + h0 + HEAD_DIM] = yh.astype(out_ref.dtype)

    hn = hn_s[...]
    sg_ref[...] = _silu(_dot(hn, w_ref[:, _C_GOUT:_C_GOUT + KEY_WIDTH])).astype(sg_ref.dtype)

    ab = _dot(hn, wab_ref[...])
    g = -jnp.exp(alog_ref[...]) * _softplus(ab + dtb_ref[...])
    tg = min(INPROJ_GATE_ROWS, tm)
    r = lax.broadcasted_iota(jnp.int32, (tg, tg), 0)
    c = lax.broadcasted_iota(jnp.int32, (tg, tg), 1)
    tri = jnp.where((r // CHUNK == c // CHUNK) & (c <= r), 1.0, 0.0).astype(BF16)
    lane = lax.broadcasted_iota(jnp.int32, (tg, LANES), 1)
    for r0 in range(0, tm, tg):
        g_hi, g_mid, g_lo = _split3(g[r0:r0 + tg, :])
        gc = _dot(tri, g_hi) + _dot(tri, g_mid) + _dot(tri, g_lo)
        gb_ref[r0:r0 + tg, :] = jnp.where(lane < N_HEADS, gc, _sigmoid(ab[r0:r0 + tg, :]))
        gct = gc.T
        for p in range(tg // LANES):
            seg = gct[0:N_HEADS, p * LANES:(p + 1) * LANES]
            pair = (r0 + p * LANES) // CHUNK
            gct_ref[pair] = seg
            gct_ref[pair + 1] = pltpu.roll(seg, CHUNK, axis=1)

    val = _dot(hn, w_ref[:, _C_VAL:_C_VAL + D_MODEL])
    gt = _dot(hn, w_ref[:, _C_GT:_C_GT + D_MODEL])
    c_ref[...] = (val * _sigmoid(gt)).astype(c_ref.dtype)
    for c0 in range(0, 2 * D_MODEL, D_MODEL):
        gates_ref[:, c0:c0 + D_MODEL] = _dot(hn, w_ref[:, _C_GATES + c0:_C_GATES + c0 + D_MODEL]).astype(gates_ref.dtype)


def _inproj(x, nw, w_main, w_ab, conv_w, alog_pad, dtb_pad, seq):
    n = x.shape[0]
    tm = min(INPROJ_TM, seq)
    nch = tm // CHUNK
    row = lambda width: pl.BlockSpec((tm, width), lambda i: (i, 0))
    return pl.pallas_call(
        functools.partial(_inproj_kernel, tiles_per_seq=seq // tm),
        grid=(n // tm,),
        in_specs=[row(D_MODEL), _resident((1, D_MODEL)), _resident((D_MODEL, _C_END)),
                  _resident((D_MODEL, LANES)), _resident((SHORT_CONV, 3 * KEY_WIDTH)),
                  _resident((1, LANES)), _resident((1, LANES))],
        out_specs=[row(KEY_WIDTH), row(KEY_WIDTH), row(KEY_WIDTH), row(KEY_WIDTH), row(LANES),
                   pl.BlockSpec((nch, N_HEADS, LANES), lambda i: (i, 0, 0)),
                   row(D_MODEL), row(2 * D_MODEL)],
        out_shape=[
            jax.ShapeDtypeStruct((n, KEY_WIDTH), BF16),
            jax.ShapeDtypeStruct((n, KEY_WIDTH), BF16),
            jax.ShapeDtypeStruct((n, KEY_WIDTH), BF16),
            jax.ShapeDtypeStruct((n, KEY_WIDTH), BF16),
            jax.ShapeDtypeStruct((n, LANES), F32),
            jax.ShapeDtypeStruct((n // CHUNK, N_HEADS, LANES), F32),
            jax.ShapeDtypeStruct((n, D_MODEL), BF16),
            jax.ShapeDtypeStruct((n, 2 * D_MODEL), BF16),
        ],
        scratch_shapes=[pltpu.VMEM((tm, D_MODEL), BF16),
                        pltpu.VMEM((3, tm + SUBLANES, KEY_WIDTH), F32),
                        pltpu.VMEM((SUBLANES, 3 * KEY_WIDTH), F32)],
        compiler_params=_params(("arbitrary",)),
        name="inproj",
    )(x, nw, w_main, w_ab, conv_w, alog_pad, dtb_pad)


DN_TB = 256
DN_PA = 2


def _column(arr, lane):
    ids = lax.broadcasted_iota(jnp.int32, arr.shape, 1)
    return jnp.sum(jnp.where(ids == lane, arr, 0.0), axis=-1, keepdims=True)


def _deltanet_kernel(q_ref, k_ref, v_ref, sg_ref, gb_ref, gct_ref, hn_ref, o_ref,
                     state_s, lhs_s, bm_s, ou_s, eg_s):
    tb = q_ref.shape[0]
    nch = tb // CHUNK

    @pl.when(pl.program_id(1) == 0)
    def _():
        state_s[...] = jnp.zeros_like(state_s)

    ri = lax.broadcasted_iota(jnp.int32, (CHUNK, CHUNK), 0)
    ci = lax.broadcasted_iota(jnp.int32, (CHUNK, CHUNK), 1)
    incl = ri >= ci
    strict = ri > ci
    eye = jnp.where(ri == ci, 1.0, 0.0)
    hnw = hn_ref[...]

    def head_lanes(h):
        return slice(h * HEAD_DIM, (h + 1) * HEAD_DIM)

    def phase_a(it, carry):
        probs = [(it * DN_PA + cc, h) for cc in range(DN_PA) for h in range(N_HEADS)]
        rows_of = {}
        for cc in range(DN_PA):
            rows = pl.ds(pl.multiple_of((it * DN_PA + cc) * CHUNK, CHUNK), CHUNK)
            rows_of[cc] = (rows, gb_ref[rows, :])
        pre = []
        for i, (ch, h) in enumerate(probs):
            rows, gb_blk = rows_of[i // N_HEADS]
            q16 = q_ref[rows, head_lanes(h)]
            k16 = k_ref[rows, head_lanes(h)]
            q = q16.astype(F32)
            k = k16.astype(F32)
            v = v_ref[rows, head_lanes(h)].astype(F32)
            beta = _column(gb_blk, N_HEADS + h)
            gcc = _column(gb_blk, h)
            gcr = gct_ref[ch, pl.ds(h, 1), :][:, 0:CHUNK]
            g_last = gcc[CHUNK - 1:CHUNK, :]
            e_incl = jnp.exp(jnp.where(incl, gcc - gcr, -jnp.inf))
            egc = jnp.exp(gcc)
            kb = k * beta
            eg_s[ch * N_HEADS + h] = jnp.broadcast_to(jnp.exp(g_last), (SUBLANES, LANES))
            pre.append(dict(
                q16=q16, k16=k16, kb16=kb.astype(BF16), e_incl=e_incl,
                qd=q * egc, rhs=jnp.concatenate([v * beta, kb * egc], axis=-1).astype(BF16),
                kdt16=(k * jnp.exp(g_last - gcc)).T.astype(BF16)))

        ms = [_dot_nt(d["kb16"], d["k16"]) * jnp.where(strict, d["e_incl"], 0.0) for d in pre]
        attns = [(_dot_nt(d["q16"], d["k16"]) * d["e_incl"]).astype(BF16) for d in pre]

        ps = [-m for m in ms]
        tinvs = [eye + p for p in ps]
        for _ in range(5):
            p16s = [p.astype(BF16) for p in ps]
            ps = [_dot(p16, p16) for p16 in p16s]
            tinvs = [tv + _dot(tv.astype(BF16), p.astype(BF16)) for tv, p in zip(tinvs, ps)]

        uws = [_dot(tv.astype(BF16), d["rhs"]).astype(BF16) for tv, d in zip(tinvs, pre)]
        kuws = [_dot(d["kdt16"], uw) for d, uw in zip(pre, uws)]
        auws = [_dot(a, uw) for a, uw in zip(attns, uws)]
        for (ch, h), d, kuw, auw in zip(probs, pre, kuws, auws):
            idx = ch * N_HEADS + h
            rows = pl.ds(pl.multiple_of(ch * CHUNK, CHUNK), CHUNK)
            qeff = d["qd"] - auw[:, HEAD_DIM:2 * HEAD_DIM]
            lhs_s[idx, 0:CHUNK, :] = qeff.astype(BF16)
            lhs_s[idx, CHUNK:CHUNK + HEAD_DIM, :] = kuw[:, HEAD_DIM:2 * HEAD_DIM].astype(BF16)
            bm_s[idx] = kuw[:, 0:HEAD_DIM]
            ou_s[rows, head_lanes(h)] = auw[:, 0:HEAD_DIM]
        return carry

    lax.fori_loop(0, nch // DN_PA, phase_a, 0)

    def phase_b(ch, carry):
        rows = pl.ds(pl.multiple_of(ch * CHUNK, CHUNK), CHUNK)
        res = [_dot(lhs_s[ch * N_HEADS + h], state_s[h].astype(BF16)) for h in range(N_HEADS)]
        for h in range(N_HEADS):
            idx = ch * N_HEADS + h
            state_s[h] = (state_s[h] * eg_s[idx][0:1, :] - res[h][CHUNK:CHUNK + HEAD_DIM, :] + bm_s[idx])
            o = res[h][0:CHUNK, :] + ou_s[rows, head_lanes(h)]
            o = _rms(o, hnw) * sg_ref[rows, head_lanes(h)].astype(F32)
            o_ref[rows, head_lanes(h)] = o.astype(o_ref.dtype)
        return carry

    lax.fori_loop(0, nch, phase_b, 0)


def _deltanet(q, k, v, sg, gb, gct, head_norm, bsz, seq):
    n = q.shape[0]
    tb = min(DN_TB, seq)
    nt = seq // tb
    nch = tb // CHUNK
    row = lambda width: pl.BlockSpec((tb, width), lambda b, t: (b * nt + t, 0))
    return pl.pallas_call(
        _deltanet_kernel,
        grid=(bsz, nt),
        in_specs=[row(KEY_WIDTH), row(KEY_WIDTH), row(KEY_WIDTH), row(KEY_WIDTH), row(LANES),
                  pl.BlockSpec((nch, N_HEADS, LANES), lambda b, t: (b * nt + t, 0, 0)),
                  pl.BlockSpec((1, HEAD_DIM), lambda b, t: (0, 0))],
        out_specs=row(KEY_WIDTH),
        out_shape=jax.ShapeDtypeStruct((n, KEY_WIDTH), BF16),
        scratch_shapes=[
            pltpu.VMEM((N_HEADS, HEAD_DIM, HEAD_DIM), F32),
            pltpu.VMEM((nch * N_HEADS, CHUNK + HEAD_DIM, HEAD_DIM), BF16),
            pltpu.VMEM((nch * N_HEADS, HEAD_DIM, HEAD_DIM), F32),
            pltpu.VMEM((tb, KEY_WIDTH), F32),
            pltpu.VMEM((nch * N_HEADS, SUBLANES, LANES), F32),
        ],
        compiler_params=_params(("parallel", "arbitrary")),
        name="deltanet",
    )(q, k, v, sg, gb, gct, head_norm)


CM_TB = 256
CM_HALO = 32
CM_RB = 64


def _convmod_kernel(c_ref, w_ref, b_ref, nw_ref, o_ref, xbuf, ybuf):
    tb = c_ref.shape[0]

    @pl.when(pl.program_id(1) == 0)
    def _():
        xbuf[0:CM_HALO, :] = jnp.zeros((CM_HALO, D_MODEL), F32)

    xbuf[CM_HALO:CM_HALO + tb, :] = c_ref[...].astype(F32)
    base = CM_HALO - (DW_WIDTH - 1)
    span = CM_RB + CM_HALO

    def lane_block(lb, carry):
        lanes = pl.ds(pl.multiple_of(lb * LANES, LANES), LANES)
        for r0 in range(0, tb, CM_RB):
            xs = xbuf[r0:r0 + span, lanes]
            acc = jnp.zeros((CM_RB, LANES), F32) + b_ref[:, lanes]
            for phase in range(SUBLANES):
                taps = [i for i in range(DW_WIDTH) if (base + i) % SUBLANES == phase]
                if not taps:
                    continue
                xr = xs if phase == 0 else pltpu.roll(xs, span - phase, axis=0)
                for i in taps:
                    a0 = (base + i) - phase
                    acc = acc + w_ref[pl.ds(i, 1), lanes] * xr[a0:a0 + CM_RB, :]
            ybuf[r0:r0 + CM_RB, lanes] = acc
        return carry

    lax.fori_loop(0, D_MODEL // LANES, lane_block, 0)
    xbuf[0:CM_HALO, :] = xbuf[tb:tb + CM_HALO, :]
    o_ref[...] = _silu(_rms(ybuf[...], nw_ref[...])).astype(o_ref.dtype)


def _convmod(c, w, b, nw, bsz, seq):
    n = c.shape[0]
    tb = min(CM_TB, seq)
    nt = seq // tb
    row = pl.BlockSpec((tb, D_MODEL), lambda bi, t: (bi * nt + t, 0))
    return pl.pallas_call(
        _convmod_kernel,
        grid=(bsz, nt),
        in_specs=[row,
                  pl.BlockSpec((DW_WIDTH, D_MODEL), lambda bi, t: (0, 0)),
                  pl.BlockSpec((1, D_MODEL), lambda bi, t: (0, 0)),
                  pl.BlockSpec((1, D_MODEL), lambda bi, t: (0, 0))],
        out_specs=row,
        out_shape=jax.ShapeDtypeStruct((n, D_MODEL), BF16),
        scratch_shapes=[pltpu.VMEM((tb + CM_HALO, D_MODEL), F32), pltpu.VMEM((tb, D_MODEL), F32)],
        compiler_params=_params(("parallel", "arbitrary")),
        name="convmod",
    )(c, w, b, nw)


OUT_TM = 512


def _outproj_kernel(x_ref, oa_ref, cb_ref, gates_ref, wa_ref, wb_ref, wo_ref, o_ref):
    ya = _dot(oa_ref[...], wa_ref[...])
    yb = _dot(cb_ref[...], wb_ref[...])
    merged = (_sigmoid(gates_ref[:, 0:D_MODEL].astype(F32)) * ya
              + _sigmoid(gates_ref[:, D_MODEL:2 * D_MODEL].astype(F32)) * yb)
    o_ref[...] = x_ref[...] + _dot(merged.astype(BF16), wo_ref[...])


def _outproj(x, oa, cb, gates, wa, wb, wo):
    n = x.shape[0]
    tm = min(OUT_TM, n)
    row = lambda width: pl.BlockSpec((tm, width), lambda i: (i, 0))
    wspec = pl.BlockSpec((D_MODEL, D_MODEL), lambda i: (0, 0))
    return pl.pallas_call(
        _outproj_kernel,
        grid=(n // tm,),
        in_specs=[row(D_MODEL), row(KEY_WIDTH), row(D_MODEL), row(2 * D_MODEL), wspec, wspec, wspec],
        out_specs=row(D_MODEL),
        out_shape=jax.ShapeDtypeStruct((n, D_MODEL), F32),
        compiler_params=_params(("parallel",)),
        name="outproj",
    )(x, oa, cb, gates, wa, wb, wo)


def _pad_lanes(v, offset=0):
    return jnp.zeros((1, LANES), F32).at[0, offset:offset + v.shape[0]].set(v.astype(F32))


def kernel(x, ffn1_norm, ffn1_w_gate, ffn1_w_up, ffn1_w_down, mix_norm, w_in, qkv_conv_w, a_log, dt_bias, head_norm, w_a_out, dw_conv_w, dw_conv_b, conv_norm, w_b_out, w_o, ffn2_norm, ffn2_w_gate, ffn2_w_up, ffn2_w_down, final_norm):
    bsz, seq, _ = x.shape
    depth = w_in.shape[0]
    h = x.reshape(bsz * seq, D_MODEL)
    fw = final_norm.reshape(1, D_MODEL)
    n_ab = 2 * N_HEADS
    c_ab = 4 * KEY_WIDTH
    for l in range(depth):
        h = _ffn(h, ffn1_norm[l].reshape(1, -1), ffn1_w_gate[l].astype(BF16), ffn1_w_up[l].astype(BF16),
                 ffn1_w_down[l].astype(BF16), fw, final=False)
        w_main = jnp.concatenate([w_in[l][:, :c_ab], w_in[l][:, c_ab + n_ab:]], axis=1).astype(BF16)
        w_ab = jnp.pad(w_in[l][:, c_ab:c_ab + n_ab], ((0, 0), (0, LANES - n_ab))).astype(BF16)
        q, k, v, sg, gb, gct, c, gates = _inproj(
            h, mix_norm[l].reshape(1, -1), w_main, w_ab, qkv_conv_w[l],
            _pad_lanes(a_log[l]), _pad_lanes(dt_bias[l]), seq)
        oa = _deltanet(q, k, v, sg, gb, gct, head_norm[l].reshape(1, -1), bsz, seq)
        cb = _convmod(c, dw_conv_w[l], dw_conv_b[l].reshape(1, -1), conv_norm[l].reshape(1, -1), bsz, seq)
        h = _outproj(h, oa, cb, gates, w_a_out[l].astype(BF16), w_b_out[l].astype(BF16), w_o[l].astype(BF16))
        h = _ffn(h, ffn2_norm[l].reshape(1, -1), ffn2_w_gate[l].astype(BF16), ffn2_w_up[l].astype(BF16),
                 ffn2_w_down[l].astype(BF16), fw, final=(l == depth - 1))
    return h.reshape(bsz, seq, D_MODEL)
```

```python
import functools

import jax
import jax.numpy as jnp
from jax import lax
from jax.experimental import pallas as pl
from jax.experimental.pallas import tpu as pltpu

D_MODEL = 1024
N_HEADS = 8
HEAD_DIM = 128
KEY_WIDTH = N_HEADS * HEAD_DIM
CHUNK = 64
SHORT_CONV = 4
DW_WIDTH = 31
D_FF = 2816
EPS = 1e-6

LANES = 128
SUBLANES = 8
VMEM_LIMIT = 56 * 1024 * 1024

F32 = jnp.float32
BF16 = jnp.bfloat16


def _rms(x, w):
    return x * lax.rsqrt(jnp.mean(x * x, axis=-1, keepdims=True) + EPS) * w


def _sigmoid(x):
    return 1.0 / (1.0 + jnp.exp(-x))


def _silu(x):
    return x * _sigmoid(x)


def _softplus(x):
    return jnp.maximum(x, 0.0) + jnp.log1p(jnp.exp(-jnp.abs(x)))


def _dot(a, b):
    return jnp.dot(a, b, preferred_element_type=F32)


def _dot_nt(a, b):
    return lax.dot_general(a, b, (((1,), (1,)), ((), ())), preferred_element_type=F32)


def _params(sem):
    return pltpu.CompilerParams(dimension_semantics=sem, vmem_limit_bytes=VMEM_LIMIT)


def _resident(shape):
    nd = len(shape)
    return pl.BlockSpec(shape, lambda *_: (0,) * nd, pipeline_mode=pl.Buffered(1))


FFN_TM = 1024
FFN_CW = 256


def _ffn_kernel(x_ref, nw_ref, wg_ref, wu_ref, wd_ref, fw_ref, o_ref, hn_s, a_s, *, final):
    hn_s[...] = _rms(x_ref[...], nw_ref[...]).astype(BF16)
    for c0 in range(0, D_FF, FFN_CW):
        hn = hn_s[...]
        g = _dot(hn, wg_ref[:, c0:c0 + FFN_CW])
        u = _dot(hn, wu_ref[:, c0:c0 + FFN_CW])
        a_s[:, c0:c0 + FFN_CW] = (_silu(g) * u).astype(BF16)
    y = x_ref[...] + 0.5 * _dot(a_s[...], wd_ref[...])
    if final:
        y = _rms(y, fw_ref[...])
    o_ref[...] = y


def _ffn(x, nw, wg, wu, wd, fw, final):
    n = x.shape[0]
    tm = min(FFN_TM, n)
    row = pl.BlockSpec((tm, D_MODEL), lambda i: (i, 0))
    return pl.pallas_call(
        functools.partial(_ffn_kernel, final=final),
        grid=(n // tm,),
        in_specs=[row, _resident((1, D_MODEL)), _resident((D_MODEL, D_FF)), _resident((D_MODEL, D_FF)),
                  _resident((D_FF, D_MODEL)), _resident((1, D_MODEL))],
        out_specs=row,
        out_shape=jax.ShapeDtypeStruct((n, D_MODEL), F32),
        scratch_shapes=[pltpu.VMEM((tm, D_MODEL), BF16), pltpu.VMEM((tm, D_FF), BF16)],
        compiler_params=_params(("parallel",)),
        name="ffn",
    )(x, nw, wg, wu, wd, fw)


INPROJ_TM = 512
INPROJ_GATE_ROWS = 256
INPROJ_CW = 256
INPROJ_RB = 64
_C_GOUT = 3 * KEY_WIDTH
_C_VAL = _C_GOUT + KEY_WIDTH
_C_GT = _C_VAL + D_MODEL
_C_GATES = _C_GT + D_MODEL
_C_END = _C_GATES + 2 * D_MODEL


def _split3(x):
    hi = x.astype(BF16)
    r1 = x - hi.astype(F32)
    mid = r1.astype(BF16)
    lo = (r1 - mid.astype(F32)).astype(BF16)
    return hi, mid, lo


def _inproj_kernel(x_ref, nw_ref, w_ref, wab_ref, cw_ref, alog_ref, dtb_ref,
                   q_ref, k_ref, v_ref, sg_ref, gb_ref, gct_ref, c_ref, gates_ref,
                   hn_s, xb, halo_s, *, tiles_per_seq):
    tm = x_ref.shape[0]
    cw = INPROJ_CW
    rb = INPROJ_RB

    @pl.when(pl.program_id(0) % tiles_per_seq == 0)
    def _():
        halo_s[...] = jnp.zeros_like(halo_s)

    hn_s[...] = _rms(x_ref[...], nw_ref[...]).astype(BF16)

    for group, out_ref in enumerate((q_ref, k_ref, v_ref)):
        g0 = group * KEY_WIDTH
        xc = xb.at[group]
        xc[0:SUBLANES, :] = halo_s[:, g0:g0 + KEY_WIDTH]
        xc[SUBLANES:SUBLANES + tm, :] = _dot(hn_s[...], w_ref[:, g0:g0 + KEY_WIDTH])
        halo_s[:, g0:g0 + KEY_WIDTH] = xc[tm:tm + SUBLANES, :]
    for group, out_ref in enumerate((q_ref, k_ref, v_ref)):
        g0 = group * KEY_WIDTH
        xc = xb.at[group]
        for c0 in range(0, KEY_WIDTH, cw):
            for r0 in range(0, tm, rb):
                xs = xc[r0:r0 + rb + SUBLANES, c0:c0 + cw]
                z = cw_ref[0:1, g0 + c0:g0 + c0 + cw] * xs
                for i in range(1, SHORT_CONV):
                    z = pltpu.roll(z, 1, axis=0) + cw_ref[i:i + 1, g0 + c0:g0 + c0 + cw] * xs
                y = _silu(z[SUBLANES:, :])
                for h0 in range(0, cw, HEAD_DIM):
                    yh = y[:, h0:h0 + HEAD_DIM]
                    if group == 0:
                        yh = yh * lax.rsqrt(jnp.sum(yh * yh, axis=-1, keepdims=True) + EPS) * (HEAD_DIM ** -0.5)
                    elif group == 1:
                        yh = yh * lax.rsqrt(jnp.sum(yh * yh, axis=-1, keepdims=True) + EPS)
                    out_ref[r0:r0 + rb, c0 + h0:c0 + h0 + HEAD_DIM] = yh.astype(out_ref.dtype)

    hn = hn_s[...]
    sg_ref[...] = _silu(_dot(hn, w_ref[:, _C_GOUT:_C_GOUT + KEY_WIDTH])).astype(sg_ref.dtype)

    ab = _dot(hn, wab_ref[...])
    g = -jnp.exp(alog_ref[...]) * _softplus(ab + dtb_ref[...])
    tg = min(INPROJ_GATE_ROWS, tm)
    r = lax.broadcasted_iota(jnp.int32, (tg, tg), 0)
    c = lax.broadcasted_iota(jnp.int32, (tg, tg), 1)
    tri = jnp.where((r // CHUNK == c // CHUNK) & (c <= r), 1.0, 0.0).astype(BF16)
    lane = lax.broadcasted_iota(jnp.int32, (tg, LANES), 1)
    for r0 in range(0, tm, tg):
        g_hi, g_mid, g_lo = _split3(g[r0:r0 + tg, :])
        gc = _dot(tri, g_hi) + _dot(tri, g_mid) + _dot(tri, g_lo)
        gb_ref[r0:r0 + tg, :] = jnp.where(lane < N_HEADS, gc, _sigmoid(ab[r0:r0 + tg, :]))
        gct = gc.T
        for p in range(tg // LANES):
            seg = gct[0:N_HEADS, p * LANES:(p + 1) * LANES]
            pair = (r0 + p * LANES) // CHUNK
            gct_ref[pair] = seg
            gct_ref[pair + 1] = pltpu.roll(seg, CHUNK, axis=1)

    val = _dot(hn, w_ref[:, _C_VAL:_C_VAL + D_MODEL])
    gt = _dot(hn, w_ref[:, _C_GT:_C_GT + D_MODEL])
    c_ref[...] = (val * _sigmoid(gt)).astype(c_ref.dtype)
    for c0 in range(0, 2 * D_MODEL, D_MODEL):
        gates_ref[:, c0:c0 + D_MODEL] = _dot(hn, w_ref[:, _C_GATES + c0:_C_GATES + c0 + D_MODEL]).astype(gates_ref.dtype)


def _inproj(x, nw, w_main, w_ab, conv_w, alog_pad, dtb_pad, seq):
    n = x.shape[0]
    tm = min(INPROJ_TM, seq)
    nch = tm // CHUNK
    row = lambda width: pl.BlockSpec((tm, width), lambda i: (i, 0))
    return pl.pallas_call(
        functools.partial(_inproj_kernel, tiles_per_seq=seq // tm),
        grid=(n // tm,),
        in_specs=[row(D_MODEL), _resident((1, D_MODEL)), _resident((D_MODEL, _C_END)),
                  _resident((D_MODEL, LANES)), _resident((SHORT_CONV, 3 * KEY_WIDTH)),
                  _resident((1, LANES)), _resident((1, LANES))],
        out_specs=[row(KEY_WIDTH), row(KEY_WIDTH), row(KEY_WIDTH), row(KEY_WIDTH), row(LANES),
                   pl.BlockSpec((nch, N_HEADS, LANES), lambda i: (i, 0, 0)),
                   row(D_MODEL), row(2 * D_MODEL)],
        out_shape=[
            jax.ShapeDtypeStruct((n, KEY_WIDTH), BF16),
            jax.ShapeDtypeStruct((n, KEY_WIDTH), BF16),
            jax.ShapeDtypeStruct((n, KEY_WIDTH), BF16),
            jax.ShapeDtypeStruct((n, KEY_WIDTH), BF16),
            jax.ShapeDtypeStruct((n, LANES), F32),
            jax.ShapeDtypeStruct((n // CHUNK, N_HEADS, LANES), F32),
            jax.ShapeDtypeStruct((n, D_MODEL), BF16),
            jax.ShapeDtypeStruct((n, 2 * D_MODEL), BF16),
        ],
        scratch_shapes=[pltpu.VMEM((tm, D_MODEL), BF16),
                        pltpu.VMEM((3, tm + SUBLANES, KEY_WIDTH), F32),
                        pltpu.VMEM((SUBLANES, 3 * KEY_WIDTH), F32)],
        compiler_params=_params(("arbitrary",)),
        name="inproj",
    )(x, nw, w_main, w_ab, conv_w, alog_pad, dtb_pad)


DN_TB = 256
DN_PA = 2


def _column(arr, lane):
    ids = lax.broadcasted_iota(jnp.int32, arr.shape, 1)
    return jnp.sum(jnp.where(ids == lane, arr, 0.0), axis=-1, keepdims=True)


def _deltanet_kernel(q_ref, k_ref, v_ref, sg_ref, gb_ref, gct_ref, hn_ref, o_ref,
                     state_s, lhs_s, bm_s, ou_s, eg_s):
    tb = q_ref.shape[0]
    nch = tb // CHUNK

    @pl.when(pl.program_id(1) == 0)
    def _():
        state_s[...] = jnp.zeros_like(state_s)

    ri = lax.broadcasted_iota(jnp.int32, (CHUNK, CHUNK), 0)
    ci = lax.broadcasted_iota(jnp.int32, (CHUNK, CHUNK), 1)
    incl = ri >= ci
    strict = ri > ci
    eye = jnp.where(ri == ci, 1.0, 0.0)
    hnw = hn_ref[...]

    def head_lanes(h):
        return slice(h * HEAD_DIM, (h + 1) * HEAD_DIM)

    def phase_a(it, carry):
        probs = [(it * DN_PA + cc, h) for cc in range(DN_PA) for h in range(N_HEADS)]
        rows_of = {}
        for cc in range(DN_PA):
            rows = pl.ds(pl.multiple_of((it * DN_PA + cc) * CHUNK, CHUNK), CHUNK)
            rows_of[cc] = (rows, gb_ref[rows, :])
        pre = []
        for i, (ch, h) in enumerate(probs):
            rows, gb_blk = rows_of[i // N_HEADS]
            q16 = q_ref[rows, head_lanes(h)]
            k16 = k_ref[rows, head_lanes(h)]
            q = q16.astype(F32)
            k = k16.astype(F32)
            v = v_ref[rows, head_lanes(h)].astype(F32)
            beta = _column(gb_blk, N_HEADS + h)
            gcc = _column(gb_blk, h)
            gcr = gct_ref[ch, pl.ds(h, 1), :][:, 0:CHUNK]
            g_last = gcc[CHUNK - 1:CHUNK, :]
            e_incl = jnp.exp(jnp.where(incl, gcc - gcr, -jnp.inf))
            egc = jnp.exp(gcc)
            kb = k * beta
            eg_s[ch * N_HEADS + h] = jnp.broadcast_to(jnp.exp(g_last), (SUBLANES, LANES))
            pre.append(dict(
                q16=q16, k16=k16, kb16=kb.astype(BF16), e_incl=e_incl,
                qd=q * egc, rhs=jnp.concatenate([v * beta, kb * egc], axis=-1).astype(BF16),
                kdt16=(k * jnp.exp(g_last - gcc)).T.astype(BF16)))

        ms = [_dot_nt(d["kb16"], d["k16"]) * jnp.where(strict, d["e_incl"], 0.0) for d in pre]
        attns = [(_dot_nt(d["q16"], d["k16"]) * d["e_incl"]).astype(BF16) for d in pre]

        ps = [-m for m in ms]
        tinvs = [eye + p for p in ps]
        for _ in range(5):
            p16s = [p.astype(BF16) for p in ps]
            ps = [_dot(p16, p16) for p16 in p16s]
            tinvs = [tv + _dot(tv.astype(BF16), p.astype(BF16)) for tv, p in zip(tinvs, ps)]

        uws = [_dot(tv.astype(BF16), d["rhs"]).astype(BF16) for tv, d in zip(tinvs, pre)]
        kuws = [_dot(d["kdt16"], uw) for d, uw in zip(pre, uws)]
        auws = [_dot(a, uw) for a, uw in zip(attns, uws)]
        for (ch, h), d, kuw, auw in zip(probs, pre, kuws, auws):
            idx = ch * N_HEADS + h
            rows = pl.ds(pl.multiple_of(ch * CHUNK, CHUNK), CHUNK)
            qeff = d["qd"] - auw[:, HEAD_DIM:2 * HEAD_DIM]
            lhs_s[idx, 0:CHUNK, :] = qeff.astype(BF16)
            lhs_s[idx, CHUNK:CHUNK + HEAD_DIM, :] = kuw[:, HEAD_DIM:2 * HEAD_DIM].astype(BF16)
            bm_s[idx] = kuw[:, 0:HEAD_DIM]
            ou_s[rows, head_lanes(h)] = auw[:, 0:HEAD_DIM]
        return carry

    lax.fori_loop(0, nch // DN_PA, phase_a, 0)

    def phase_b(ch, carry):
        rows = pl.ds(pl.multiple_of(ch * CHUNK, CHUNK), CHUNK)
        res = [_dot(lhs_s[ch * N_HEADS + h], state_s[h].astype(BF16)) for h in range(N_HEADS)]
        for h in range(N_HEADS):
            idx = ch * N_HEADS + h
            state_s[h] = (state_s[h] * eg_s[idx][0:1, :] - res[h][CHUNK:CHUNK + HEAD_DIM, :] + bm_s[idx])
            o = res[h][0:CHUNK, :] + ou_s[rows, head_lanes(h)]
            o = _rms(o, hnw) * sg_ref[rows, head_lanes(h)].astype(F32)
            o_ref[rows, head_lanes(h)] = o.astype(o_ref.dtype)
        return carry

    lax.fori_loop(0, nch, phase_b, 0)


def _deltanet(q, k, v, sg, gb, gct, head_norm, bsz, seq):
    n = q.shape[0]
    tb = min(DN_TB, seq)
    nt = seq // tb
    nch = tb // CHUNK
    row = lambda width: pl.BlockSpec((tb, width), lambda b, t: (b * nt + t, 0))
    return pl.pallas_call(
        _deltanet_kernel,
        grid=(bsz, nt),
        in_specs=[row(KEY_WIDTH), row(KEY_WIDTH), row(KEY_WIDTH), row(KEY_WIDTH), row(LANES),
                  pl.BlockSpec((nch, N_HEADS, LANES), lambda b, t: (b * nt + t, 0, 0)),
                  pl.BlockSpec((1, HEAD_DIM), lambda b, t: (0, 0))],
        out_specs=row(KEY_WIDTH),
        out_shape=jax.ShapeDtypeStruct((n, KEY_WIDTH), BF16),
        scratch_shapes=[
            pltpu.VMEM((N_HEADS, HEAD_DIM, HEAD_DIM), F32),
            pltpu.VMEM((nch * N_HEADS, CHUNK + HEAD_DIM, HEAD_DIM), BF16),
            pltpu.VMEM((nch * N_HEADS, HEAD_DIM, HEAD_DIM), F32),
            pltpu.VMEM((tb, KEY_WIDTH), F32),
            pltpu.VMEM((nch * N_HEADS, SUBLANES, LANES), F32),
        ],
        compiler_params=_params(("parallel", "arbitrary")),
        name="deltanet",
    )(q, k, v, sg, gb, gct, head_norm)


CM_TB = 256
CM_HALO = 32
CM_RB = 64


def _convmod_kernel(c_ref, w_ref, b_ref, nw_ref, o_ref, xbuf, ybuf):
    tb = c_ref.shape[0]

    @pl.when(pl.program_id(1) == 0)
    def _():
        xbuf[0:CM_HALO, :] = jnp.zeros((CM_HALO, D_MODEL), F32)

    xbuf[CM_HALO:CM_HALO + tb, :] = c_ref[...].astype(F32)
    base = CM_HALO - (DW_WIDTH - 1)
    span = CM_RB + CM_HALO

    def lane_block(lb, carry):
        lanes = pl.ds(pl.multiple_of(lb * LANES, LANES), LANES)
        for r0 in range(0, tb, CM_RB):
            xs = xbuf[r0:r0 + span, lanes]
            acc = jnp.zeros((CM_RB, LANES), F32) + b_ref[:, lanes]
            for phase in range(SUBLANES):
                taps = [i for i in range(DW_WIDTH) if (base + i) % SUBLANES == phase]
                if not taps:
                    continue
                xr = xs if phase == 0 else pltpu.roll(xs, span - phase, axis=0)
                for i in taps:
                    a0 = (base + i) - phase
                    acc = acc + w_ref[pl.ds(i, 1), lanes] * xr[a0:a0 + CM_RB, :]
            ybuf[r0:r0 + CM_RB, lanes] = acc
        return carry

    lax.fori_loop(0, D_MODEL // LANES, lane_block, 0)
    xbuf[0:CM_HALO, :] = xbuf[tb:tb + CM_HALO, :]
    o_ref[...] = _silu(_rms(ybuf[...], nw_ref[...])).astype(o_ref.dtype)


def _convmod(c, w, b, nw, bsz, seq):
    n = c.shape[0]
    tb = min(CM_TB, seq)
    nt = seq // tb
    row = pl.BlockSpec((tb, D_MODEL), lambda bi, t: (bi * nt + t, 0))
    return pl.pallas_call(
        _convmod_kernel,
        grid=(bsz, nt),
        in_specs=[row,
                  pl.BlockSpec((DW_WIDTH, D_MODEL), lambda bi, t: (0, 0)),
                  pl.BlockSpec((1, D_MODEL), lambda bi, t: (0, 0)),
                  pl.BlockSpec((1, D_MODEL), lambda bi, t: (0, 0))],
        out_specs=row,
        out_shape=jax.ShapeDtypeStruct((n, D_MODEL), BF16),
        scratch_shapes=[pltpu.VMEM((tb + CM_HALO, D_MODEL), F32), pltpu.VMEM((tb, D_MODEL), F32)],
        compiler_params=_params(("parallel", "arbitrary")),
        name="convmod",
    )(c, w, b, nw)


OUT_TM = 512


def _outproj_kernel(x_ref, oa_ref, cb_ref, gates_ref, wa_ref, wb_ref, wo_ref, o_ref):
    ya = _dot(oa_ref[...], wa_ref[...])
    yb = _dot(cb_ref[...], wb_ref[...])
    merged = (_sigmoid(gates_ref[:, 0:D_MODEL].astype(F32)) * ya
              + _sigmoid(gates_ref[:, D_MODEL:2 * D_MODEL].astype(F32)) * yb)
    o_ref[...] = x_ref[...] + _dot(merged.astype(BF16), wo_ref[...])


def _outproj(x, oa, cb, gates, wa, wb, wo):
    n = x.shape[0]
    tm = min(OUT_TM, n)
    row = lambda width: pl.BlockSpec((tm, width), lambda i: (i, 0))
    wspec = pl.BlockSpec((D_MODEL, D_MODEL), lambda i: (0, 0))
    return pl.pallas_call(
        _outproj_kernel,
        grid=(n // tm,),
        in_specs=[row(D_MODEL), row(KEY_WIDTH), row(D_MODEL), row(2 * D_MODEL), wspec, wspec, wspec],
        out_specs=row(D_MODEL),
        out_shape=jax.ShapeDtypeStruct((n, D_MODEL), F32),
        compiler_params=_params(("parallel",)),
        name="outproj",
    )(x, oa, cb, gates, wa, wb, wo)


def _pad_lanes(v, offset=0):
    return jnp.zeros((1, LANES), F32).at[0, offset:offset + v.shape[0]].set(v.astype(F32))


def kernel(x, ffn1_norm, ffn1_w_gate, ffn1_w_up, ffn1_w_down, mix_norm, w_in, qkv_conv_w, a_log, dt_bias, head_norm, w_a_out, dw_conv_w, dw_conv_b, conv_norm, w_b_out, w_o, ffn2_norm, ffn2_w_gate, ffn2_w_up, ffn2_w_down, final_norm):
    bsz, seq, _ = x.shape
    depth = w_in.shape[0]
    h = x.reshape(bsz * seq, D_MODEL)
    fw = final_norm.reshape(1, D_MODEL)
    n_ab = 2 * N_HEADS
    c_ab = 4 * KEY_WIDTH
    for l in range(depth):
        h = _ffn(h, ffn1_norm[l].reshape(1, -1), ffn1_w_gate[l].astype(BF16), ffn1_w_up[l].astype(BF16),
                 ffn1_w_down[l].astype(BF16), fw, final=False)
        w_main = jnp.concatenate([w_in[l][:, :c_ab], w_in[l][:, c_ab + n_ab:]], axis=1).astype(BF16)
        w_ab = jnp.pad(w_in[l][:, c_ab:c_ab + n_ab], ((0, 0), (0, LANES - n_ab))).astype(BF16)
        q, k, v, sg, gb, gct, c, gates = _inproj(
            h, mix_norm[l].reshape(1, -1), w_main, w_ab, qkv_conv_w[l],
            _pad_lanes(a_log[l]), _pad_lanes(dt_bias[l]), seq)
        oa = _deltanet(q, k, v, sg, gb, gct, head_norm[l].reshape(1, -1), bsz, seq)
        cb = _convmod(c, dw_conv_w[l], dw_conv_b[l].reshape(1, -1), conv_norm[l].reshape(1, -1), bsz, seq)
        h = _outproj(h, oa, cb, gates, w_a_out[l].astype(BF16), w_b_out[l].astype(BF16), w_o[l].astype(BF16))
        h = _ffn(h, ffn2_norm[l].reshape(1, -1), ffn2_w_gate[l].astype(BF16), ffn2_w_up[l].astype(BF16),
                 ffn2_w_down[l].astype(BF16), fw, final=(l == depth - 1))
    return h.reshape(bsz, seq, D_MODEL)
```

```python
import functools

import jax
import jax.numpy as jnp
from jax import lax
from jax.experimental import pallas as pl
from jax.experimental.pallas import tpu as pltpu

D_MODEL = 1024
N_HEADS = 8
HEAD_DIM = 128
KEY_WIDTH = N_HEADS * HEAD_DIM
CHUNK = 64
SHORT_CONV = 4
DW_WIDTH = 31
D_FF = 2816
EPS = 1e-6

LANES = 128
SUBLANES = 8
VMEM_LIMIT = 56 * 1024 * 1024

F32 = jnp.float32
BF16 = jnp.bfloat16


def _rms(x, w):
    return x * lax.rsqrt(jnp.mean(x * x, axis=-1, keepdims=True) + EPS) * w


def _sigmoid(x):
    return 1.0 / (1.0 + jnp.exp(-x))


def _silu(x):
    return x * _sigmoid(x)


def _softplus(x):
    return jnp.maximum(x, 0.0) + jnp.log1p(jnp.exp(-jnp.abs(x)))


def _dot(a, b):
    return jnp.dot(a, b, preferred_element_type=F32)


def _dot_nt(a, b):
    return lax.dot_general(a, b, (((1,), (1,)), ((), ())), preferred_element_type=F32)


def _params(sem):
    return pltpu.CompilerParams(dimension_semantics=sem, vmem_limit_bytes=VMEM_LIMIT)


def _resident(shape):
    nd = len(shape)
    return pl.BlockSpec(shape, lambda *_: (0,) * nd, pipeline_mode=pl.Buffered(1))


FFN_TM = 1024
FFN_CW = 256


def _ffn_kernel(x_ref, nw_ref, wg_ref, wu_ref, wd_ref, fw_ref, o_ref, hn_s, a_s, *, final):
    hn_s[...] = _rms(x_ref[...], nw_ref[...]).astype(BF16)
    for c0 in range(0, D_FF, FFN_CW):
        hn = hn_s[...]
        g = _dot(hn, wg_ref[:, c0:c0 + FFN_CW])
        u = _dot(hn, wu_ref[:, c0:c0 + FFN_CW])
        a_s[:, c0:c0 + FFN_CW] = (_silu(g) * u).astype(BF16)
    y = x_ref[...] + 0.5 * _dot(a_s[...], wd_ref[...])
    if final:
        y = _rms(y, fw_ref[...])
    o_ref[...] = y


def _ffn(x, nw, wg, wu, wd, fw, final):
    n = x.shape[0]
    tm = min(FFN_TM, n)
    row = pl.BlockSpec((tm, D_MODEL), lambda i: (i, 0))
    return pl.pallas_call(
        functools.partial(_ffn_kernel, final=final),
        grid=(n // tm,),
        in_specs=[row, _resident((1, D_MODEL)), _resident((D_MODEL, D_FF)), _resident((D_MODEL, D_FF)),
                  _resident((D_FF, D_MODEL)), _resident((1, D_MODEL))],
        out_specs=row,
        out_shape=jax.ShapeDtypeStruct((n, D_MODEL), F32),
        scratch_shapes=[pltpu.VMEM((tm, D_MODEL), BF16), pltpu.VMEM((tm, D_FF), BF16)],
        compiler_params=_params(("parallel",)),
        name="ffn",
    )(x, nw, wg, wu, wd, fw)


INPROJ_TM = 512
INPROJ_GATE_ROWS = 256
INPROJ_CW = 256
INPROJ_RB = 64
_C_GOUT = 3 * KEY_WIDTH
_C_VAL = _C_GOUT + KEY_WIDTH
_C_GT = _C_VAL + D_MODEL
_C_GATES = _C_GT + D_MODEL
_C_END = _C_GATES + 2 * D_MODEL


def _split3(x):
    hi = x.astype(BF16)
    r1 = x - hi.astype(F32)
    mid = r1.astype(BF16)
    lo = (r1 - mid.astype(F32)).astype(BF16)
    return hi, mid, lo


def _inproj_kernel(x_ref, nw_ref, w_ref, wab_ref, cw_ref, alog_ref, dtb_ref,
                   q_ref, k_ref, v_ref, sg_ref, gb_ref, gct_ref, c_ref, gates_ref,
                   hn_s, xb, halo_s, *, tiles_per_seq):
    tm = x_ref.shape[0]
    cw = INPROJ_CW
    rb = INPROJ_RB

    @pl.when(pl.program_id(0) % tiles_per_seq == 0)
    def _():
        halo_s[...] = jnp.zeros_like(halo_s)

    hn_s[...] = _rms(x_ref[...], nw_ref[...]).astype(BF16)

    for group, out_ref in enumerate((q_ref, k_ref, v_ref)):
        g0 = group * KEY_WIDTH
        xc = xb.at[group]
        xc[0:SUBLANES, :] = halo_s[:, g0:g0 + KEY_WIDTH]
        xc[SUBLANES:SUBLANES + tm, :] = _dot(hn_s[...], w_ref[:, g0:g0 + KEY_WIDTH])
        halo_s[:, g0:g0 + KEY_WIDTH] = xc[tm:tm + SUBLANES, :]
    for group, out_ref in enumerate((q_ref, k_ref, v_ref)):
        g0 = group * KEY_WIDTH
        xc = xb.at[group]
        for c0 in range(0, KEY_WIDTH, cw):
            for r0 in range(0, tm, rb):
                xs = xc[r0:r0 + rb + SUBLANES, c0:c0 + cw]
                z = cw_ref[0:1, g0 + c0:g0 + c0 + cw] * xs
                for i in range(1, SHORT_CONV):
                    z = pltpu.roll(z, 1, axis=0) + cw_ref[i:i + 1, g0 + c0:g0 + c0 + cw] * xs
                y = _silu(z[SUBLANES:, :])
                for h0 in range(0, cw, HEAD_DIM):
                    yh = y[:, h0:h0 + HEAD_DIM]
                    if group == 0:
                        yh = yh * lax.rsqrt(jnp.sum(yh * yh, axis=-1, keepdims=True) + EPS) * (HEAD_DIM ** -0.5)
                    elif group == 1:
                        yh = yh * lax.rsqrt(jnp.sum(yh * yh, axis=-1, keepdims=True) + EPS)
                    out_ref[r0:r0 + rb, c0 + h0:c0 + h0 + HEAD_DIM] = yh.astype(out_ref.dtype)

    hn = hn_s[...]
    sg_ref[...] = _silu(_dot(hn, w_ref[:, _C_GOUT:_C_GOUT + KEY_WIDTH])).astype(sg_ref.dtype)

    ab = _dot(hn, wab_ref[...])
    g = -jnp.exp(alog_ref[...]) * _softplus(ab + dtb_ref[...])
    tg = min(INPROJ_GATE_ROWS, tm)
    r = lax.broadcasted_iota(jnp.int32, (tg, tg), 0)
    c = lax.broadcasted_iota(jnp.int32, (tg, tg), 1)
    tri = jnp.where((r // CHUNK == c // CHUNK) & (c <= r), 1.0, 0.0).astype(BF16)
    lane = lax.broadcasted_iota(jnp.int32, (tg, LANES), 1)
    for r0 in range(0, tm, tg):
        g_hi, g_mid, g_lo = _split3(g[r0:r0 + tg, :])
        gc = _dot(tri, g_hi) + _dot(tri, g_mid) + _dot(tri, g_lo)
        gb_ref[r0:r0 + tg, :] = jnp.where(lane < N_HEADS, gc, _sigmoid(ab[r0:r0 + tg, :]))
        gct = gc.T
        for p in range(tg // LANES):
            seg = gct[0:N_HEADS, p * LANES:(p + 1) * LANES]
            pair = (r0 + p * LANES) // CHUNK
            gct_ref[pair] = seg
            gct_ref[pair + 1] = pltpu.roll(seg, CHUNK, axis=1)

    val = _dot(hn, w_ref[:, _C_VAL:_C_VAL + D_MODEL])
    gt = _dot(hn, w_ref[:, _C_GT:_C_GT + D_MODEL])
    c_ref[...] = (val * _sigmoid(gt)).astype(c_ref.dtype)
    for c0 in range(0, 2 * D_MODEL, D_MODEL):
        gates_ref[:, c0:c0 + D_MODEL] = _dot(hn, w_ref[:, _C_GATES + c0:_C_GATES + c0 + D_MODEL]).astype(gates_ref.dtype)


def _inproj(x, nw, w_main, w_ab, conv_w, alog_pad, dtb_pad, seq):
    n = x.shape[0]
    tm = min(INPROJ_TM, seq)
    nch = tm // CHUNK
    row = lambda width: pl.BlockSpec((tm, width), lambda i: (i, 0))
    return pl.pallas_call(
        functools.partial(_inproj_kernel, tiles_per_seq=seq // tm),
        grid=(n // tm,),
        in_specs=[row(D_MODEL), _resident((1, D_MODEL)), _resident((D_MODEL, _C_END)),
                  _resident((D_MODEL, LANES)), _resident((SHORT_CONV, 3 * KEY_WIDTH)),
                  _resident((1, LANES)), _resident((1, LANES))],
        out_specs=[row(KEY_WIDTH), row(KEY_WIDTH), row(KEY_WIDTH), row(KEY_WIDTH), row(LANES),
                   pl.BlockSpec((nch, N_HEADS, LANES), lambda i: (i, 0, 0)),
                   row(D_MODEL), row(2 * D_MODEL)],
        out_shape=[
            jax.ShapeDtypeStruct((n, KEY_WIDTH), BF16),
            jax.ShapeDtypeStruct((n, KEY_WIDTH), BF16),
            jax.ShapeDtypeStruct((n, KEY_WIDTH), BF16),
            jax.ShapeDtypeStruct((n, KEY_WIDTH), BF16),
            jax.ShapeDtypeStruct((n, LANES), F32),
            jax.ShapeDtypeStruct((n // CHUNK, N_HEADS, LANES), F32),
            jax.ShapeDtypeStruct((n, D_MODEL), BF16),
            jax.ShapeDtypeStruct((n, 2 * D_MODEL), BF16),
        ],
        scratch_shapes=[pltpu.VMEM((tm, D_MODEL), BF16),
                        pltpu.VMEM((3, tm + SUBLANES, KEY_WIDTH), F32),
                        pltpu.VMEM((SUBLANES, 3 * KEY_WIDTH), F32)],
        compiler_params=_params(("arbitrary",)),
        name="inproj",
    )(x, nw, w_main, w_ab, conv_w, alog_pad, dtb_pad)


DN_TB = 512
DN_PA = 8
DN_G = 4


def _column(arr, lane):
    ids = lax.broadcasted_iota(jnp.int32, arr.shape, 1)
    return jnp.sum(jnp.where(ids == lane, arr, 0.0), axis=-1, keepdims=True)


def _deltanet_kernel(q_ref, k_ref, v_ref, sg_ref, gb_ref, gct_ref, hn_ref, o_ref,
                     state_s, lhs_s, bm_s, ou_s, eg_s):
    tb = q_ref.shape[0]
    nch = tb // CHUNK

    @pl.when(pl.program_id(1) == 0)
    def _():
        state_s[...] = jnp.zeros_like(state_s)

    hnw = hn_ref[...]
    gw = DN_G * CHUNK
    gl = DN_G * HEAD_DIM
    prow = lax.broadcasted_iota(jnp.int32, (CHUNK, gw), 0)
    plane = lax.broadcasted_iota(jnp.int32, (CHUNK, gw), 1)
    pcol = plane & (CHUNK - 1)
    pgrp = [plane // CHUNK == j for j in range(DN_G)]
    incl = prow >= pcol
    strict = prow > pcol
    eye = jnp.where(prow == pcol, 1.0, 0.0)
    fgrp = [lax.broadcasted_iota(jnp.int32, (CHUNK, gl), 1) // HEAD_DIM == j for j in range(DN_G)]
    lane_lo = lax.broadcasted_iota(jnp.int32, (1, LANES), 1) < CHUNK

    def head_lanes(h):
        return slice(h * HEAD_DIM, (h + 1) * HEAD_DIM)

    def per_head(cols):
        out = cols[DN_G - 1]
        for j in range(DN_G - 2, -1, -1):
            out = jnp.where(pgrp[j], cols[j], out)
        return out

    def block_diag(x, masks):
        return jnp.concatenate([jnp.where(m, x, jnp.zeros_like(x)) for m in masks], axis=0)

    def phase_a(it, carry):
        probs = [(it * DN_PA + cc, g) for cc in range(DN_PA) for g in range(N_HEADS // DN_G)]
        pre = []
        for ch, g in probs:
            rows = pl.ds(pl.multiple_of(ch * CHUNK, CHUNK), CHUNK)
            gb_blk = gb_ref[rows, :]
            lanes = slice(g * gl, (g + 1) * gl)
            q4 = q_ref[rows, lanes]
            k4 = k_ref[rows, lanes]
            k4f = k4.astype(F32)
            heads = [g * DN_G + j for j in range(DN_G)]
            beta = [_column(gb_blk, N_HEADS + h) for h in heads]
            gcc = [_column(gb_blk, h) for h in heads]
            g_last = [c[CHUNK - 1:CHUNK, :] for c in gcc]
            for h, gl_h in zip(heads, g_last):
                eg_s[ch * N_HEADS + h] = jnp.broadcast_to(jnp.exp(gl_h), (SUBLANES, LANES))
            wide = lambda cols: jnp.concatenate([jnp.broadcast_to(c, (CHUNK, HEAD_DIM)) for c in cols], axis=1)
            beta_x = wide(beta)
            egc_x = wide([jnp.exp(c) for c in gcc])
            ekd_x = wide([jnp.exp(gl_h - c) for gl_h, c in zip(g_last, gcc)])
            gct = gct_ref[ch]
            gct_hi = pltpu.roll(gct, CHUNK, axis=1)
            pair = lambda a: jnp.where(lane_lo, gct[a:a + 1, :], gct_hi[a + 1:a + 2, :])
            gcr = jnp.concatenate([pair(heads[0]), pair(heads[2])], axis=1)
            e_incl = jnp.exp(jnp.where(incl, per_head(gcc) - gcr, -jnp.inf))
            kb = k4f * beta_x
            vb16 = (v_ref[rows, lanes].astype(F32) * beta_x).astype(BF16)
            kbg16 = (kb * egc_x).astype(BF16)
            rhs = jnp.concatenate(
                [jnp.concatenate([vb16[:, head_lanes(j)], kbg16[:, head_lanes(j)]], axis=1) for j in range(DN_G)],
                axis=0)
            pre.append(dict(
                lhs_kq=jnp.concatenate([kb.astype(BF16), q4], axis=0),
                kbd=block_diag(k4, fgrp), e_incl=e_incl, rhs=rhs,
                qd=q4.astype(F32) * egc_x, kdbd=block_diag((k4f * ekd_x).astype(BF16), fgrp)))

        kqs = [_dot_nt(d["lhs_kq"], d["kbd"]) for d in pre]
        ms = [kq[0:CHUNK, :] * jnp.where(strict, d["e_incl"], 0.0) for kq, d in zip(kqs, pre)]
        attns = [(kq[CHUNK:2 * CHUNK, :] * d["e_incl"]).astype(BF16) for kq, d in zip(kqs, pre)]

        p16s = [(-m).astype(BF16) for m in ms]
        tinvs = [eye - m for m in ms]
        pbds = [block_diag(p16, pgrp) for p16 in p16s]
        for _ in range(5):
            p16s = [_dot(p16, pbd).astype(BF16) for p16, pbd in zip(p16s, pbds)]
            pbds = [block_diag(p16, pgrp) for p16 in p16s]
            tinvs = [tv + _dot(tv.astype(BF16), pbd) for tv, pbd in zip(tinvs, pbds)]

        uws = [_dot(block_diag(tv.astype(BF16), pgrp), d["rhs"]).astype(BF16)
               for tv, d in zip(tinvs, pre)]
        auws = [_dot(block_diag(a, pgrp), uw) for a, uw in zip(attns, uws)]
        kuws = [lax.dot_general(d["kdbd"], uw, (((0,), (0,)), ((), ())), preferred_element_type=F32)
                for d, uw in zip(pre, uws)]
        for (ch, g), d, kuw, auw in zip(probs, pre, kuws, auws):
            rows = pl.ds(pl.multiple_of(ch * CHUNK, CHUNK), CHUNK)
            for j in range(DN_G):
                h = g * DN_G + j
                idx = ch * N_HEADS + h
                au = auw[j * CHUNK:(j + 1) * CHUNK, :]
                ku = kuw[j * HEAD_DIM:(j + 1) * HEAD_DIM, :]
                qeff = d["qd"][:, head_lanes(j)] - au[:, HEAD_DIM:2 * HEAD_DIM]
                lhs_s[idx, 0:CHUNK, :] = qeff.astype(BF16)
                lhs_s[idx, CHUNK:CHUNK + HEAD_DIM, :] = ku[:, HEAD_DIM:2 * HEAD_DIM].astype(BF16)
                bm_s[idx] = ku[:, 0:HEAD_DIM]
                ou_s[rows, head_lanes(h)] = au[:, 0:HEAD_DIM]
        return carry

    lax.fori_loop(0, nch // DN_PA, phase_a, 0)

    def phase_b(ch, carry):
        rows = pl.ds(pl.multiple_of(ch * CHUNK, CHUNK), CHUNK)
        res = [_dot(lhs_s[ch * N_HEADS + h], state_s[h].astype(BF16)) for h in range(N_HEADS)]
        for h in range(N_HEADS):
            idx = ch * N_HEADS + h
            state_s[h] = (state_s[h] * eg_s[idx][0:1, :] - res[h][CHUNK:CHUNK + HEAD_DIM, :] + bm_s[idx])
            o = res[h][0:CHUNK, :] + ou_s[rows, head_lanes(h)]
            o = _rms(o, hnw) * sg_ref[rows, head_lanes(h)].astype(F32)
            o_ref[rows, head_lanes(h)] = o.astype(o_ref.dtype)
        return carry

    lax.fori_loop(0, nch, phase_b, 0)


def _deltanet(q, k, v, sg, gb, gct, head_norm, bsz, seq):
    n = q.shape[0]
    tb = min(DN_TB, seq)
    nt = seq // tb
    nch = tb // CHUNK
    assert seq % tb == 0 and nch % DN_PA == 0 and N_HEADS % DN_G == 0
    row = lambda width: pl.BlockSpec((tb, width), lambda b, t: (b * nt + t, 0))
    return pl.pallas_call(
        _deltanet_kernel,
        grid=(bsz, nt),
        in_specs=[row(KEY_WIDTH), row(KEY_WIDTH), row(KEY_WIDTH), row(KEY_WIDTH), row(LANES),
                  pl.BlockSpec((nch, N_HEADS, LANES), lambda b, t: (b * nt + t, 0, 0)),
                  pl.BlockSpec((1, HEAD_DIM), lambda b, t: (0, 0))],
        out_specs=row(KEY_WIDTH),
        out_shape=jax.ShapeDtypeStruct((n, KEY_WIDTH), BF16),
        scratch_shapes=[
            pltpu.VMEM((N_HEADS, HEAD_DIM, HEAD_DIM), F32),
            pltpu.VMEM((nch * N_HEADS, CHUNK + HEAD_DIM, HEAD_DIM), BF16),
            pltpu.VMEM((nch * N_HEADS, HEAD_DIM, HEAD_DIM), F32),
            pltpu.VMEM((tb, KEY_WIDTH), F32),
            pltpu.VMEM((nch * N_HEADS, SUBLANES, LANES), F32),
        ],
        compiler_params=_params(("parallel", "arbitrary")),
        name="deltanet",
    )(q, k, v, sg, gb, gct, head_norm)


CM_TB = 256
CM_HALO = 32
CM_RB = 64


def _convmod_kernel(c_ref, w_ref, b_ref, nw_ref, o_ref, xbuf, ybuf):
    tb = c_ref.shape[0]

    @pl.when(pl.program_id(1) == 0)
    def _():
        xbuf[0:CM_HALO, :] = jnp.zeros((CM_HALO, D_MODEL), F32)

    xbuf[CM_HALO:CM_HALO + tb, :] = c_ref[...].astype(F32)
    base = CM_HALO - (DW_WIDTH - 1)
    span = CM_RB + CM_HALO

    def lane_block(lb, carry):
        lanes = pl.ds(pl.multiple_of(lb * LANES, LANES), LANES)
        for r0 in range(0, tb, CM_RB):
            xs = xbuf[r0:r0 + span, lanes]
            acc = jnp.zeros((CM_RB, LANES), F32) + b_ref[:, lanes]
            for phase in range(SUBLANES):
                taps = [i for i in range(DW_WIDTH) if (base + i) % SUBLANES == phase]
                if not taps:
                    continue
                xr = xs if phase == 0 else pltpu.roll(xs, span - phase, axis=0)
                for i in taps:
                    a0 = (base + i) - phase
                    acc = acc + w_ref[pl.ds(i, 1), lanes] * xr[a0:a0 + CM_RB, :]
            ybuf[r0:r0 + CM_RB, lanes] = acc
        return carry

    lax.fori_loop(0, D_MODEL // LANES, lane_block, 0)
    xbuf[0:CM_HALO, :] = xbuf[tb:tb + CM_HALO, :]
    o_ref[...] = _silu(_rms(ybuf[...], nw_ref[...])).astype(o_ref.dtype)


def _convmod(c, w, b, nw, bsz, seq):
    n = c.shape[0]
    tb = min(CM_TB, seq)
    nt = seq // tb
    row = pl.BlockSpec((tb, D_MODEL), lambda bi, t: (bi * nt + t, 0))
    return pl.pallas_call(
        _convmod_kernel,
        grid=(bsz, nt),
        in_specs=[row,
                  pl.BlockSpec((DW_WIDTH, D_MODEL), lambda bi, t: (0, 0)),
                  pl.BlockSpec((1, D_MODEL), lambda bi, t: (0, 0)),
                  pl.BlockSpec((1, D_MODEL), lambda bi, t: (0, 0))],
        out_specs=row,
        out_shape=jax.ShapeDtypeStruct((n, D_MODEL), BF16),
        scratch_shapes=[pltpu.VMEM((tb + CM_HALO, D_MODEL), F32), pltpu.VMEM((tb, D_MODEL), F32)],
        compiler_params=_params(("parallel", "arbitrary")),
        name="convmod",
    )(c, w, b, nw)


OUT_TM = 512


def _outproj_kernel(x_ref, oa_ref, cb_ref, gates_ref, wa_ref, wb_ref, wo_ref, o_ref):
    ya = _dot(oa_ref[...], wa_ref[...])
    yb = _dot(cb_ref[...], wb_ref[...])
    merged = (_sigmoid(gates_ref[:, 0:D_MODEL].astype(F32)) * ya
              + _sigmoid(gates_ref[:, D_MODEL:2 * D_MODEL].astype(F32)) * yb)
    o_ref[...] = x_ref[...] + _dot(merged.astype(BF16), wo_ref[...])


def _outproj(x, oa, cb, gates, wa, wb, wo):
    n = x.shape[0]
    tm = min(OUT_TM, n)
    row = lambda width: pl.BlockSpec((tm, width), lambda i: (i, 0))
    wspec = pl.BlockSpec((D_MODEL, D_MODEL), lambda i: (0, 0))
    return pl.pallas_call(
        _outproj_kernel,
        grid=(n // tm,),
        in_specs=[row(D_MODEL), row(KEY_WIDTH), row(D_MODEL), row(2 * D_MODEL), wspec, wspec, wspec],
        out_specs=row(D_MODEL),
        out_shape=jax.ShapeDtypeStruct((n, D_MODEL), F32),
        compiler_params=_params(("parallel",)),
        name="outproj",
    )(x, oa, cb, gates, wa, wb, wo)


def _pad_lanes(v, offset=0):
    return jnp.zeros((1, LANES), F32).at[0, offset:offset + v.shape[0]].set(v.astype(F32))


def kernel(x, ffn1_norm, ffn1_w_gate, ffn1_w_up, ffn1_w_down, mix_norm, w_in, qkv_conv_w, a_log, dt_bias, head_norm, w_a_out, dw_conv_w, dw_conv_b, conv_norm, w_b_out, w_o, ffn2_norm, ffn2_w_gate, ffn2_w_up, ffn2_w_down, final_norm):
    bsz, seq, _ = x.shape
    depth = w_in.shape[0]
    h = x.reshape(bsz * seq, D_MODEL)
    fw = final_norm.reshape(1, D_MODEL)
    n_ab = 2 * N_HEADS
    c_ab = 4 * KEY_WIDTH
    for l in range(depth):
        h = _ffn(h, ffn1_norm[l].reshape(1, -1), ffn1_w_gate[l].astype(BF16), ffn1_w_up[l].astype(BF16),
                 ffn1_w_down[l].astype(BF16), fw, final=False)
        w_main = jnp.concatenate([w_in[l][:, :c_ab], w_in[l][:, c_ab + n_ab:]], axis=1).astype(BF16)
        w_ab = jnp.pad(w_in[l][:, c_ab:c_ab + n_ab], ((0, 0), (0, LANES - n_ab))).astype(BF16)
        q, k, v, sg, gb, gct, c, gates = _inproj(
            h, mix_norm[l].reshape(1, -1), w_main, w_ab, qkv_conv_w[l],
            _pad_lanes(a_log[l]), _pad_lanes(dt_bias[l]), seq)
        oa = _deltanet(q, k, v, sg, gb, gct, head_norm[l].reshape(1, -1), bsz, seq)
        cb = _convmod(c, dw_conv_w[l], dw_conv_b[l].reshape(1, -1), conv_norm[l].reshape(1, -1), bsz, seq)
        h = _outproj(h, oa, cb, gates, w_a_out[l].astype(BF16), w_b_out[l].astype(BF16), w_o[l].astype(BF16))
        h = _ffn(h, ffn2_norm[l].reshape(1, -1), ffn2_w_gate[l].astype(BF16), ffn2_w_up[l].astype(BF16),
                 ffn2_w_down[l].astype(BF16), fw, final=(l == depth - 1))
    return h.reshape(bsz, seq, D_MODEL)
```

```python
import functools

import jax
import jax.numpy as jnp
from jax import lax
from jax.experimental import pallas as pl
from jax.experimental.pallas import tpu as pltpu

D_MODEL = 1024
N_HEADS = 8
HEAD_DIM = 128
KEY_WIDTH = N_HEADS * HEAD_DIM
CHUNK = 64
SHORT_CONV = 4
DW_WIDTH = 31
D_FF = 2816
EPS = 1e-6

LANES = 128
SUBLANES = 8
VMEM_LIMIT = 56 * 1024 * 1024

F32 = jnp.float32
BF16 = jnp.bfloat16


def _rms(x, w):
    return x * lax.rsqrt(jnp.mean(x * x, axis=-1, keepdims=True) + EPS) * w


def _sigmoid(x):
    return 1.0 / (1.0 + jnp.exp(-x))


def _silu(x):
    return x * _sigmoid(x)


def _softplus(x):
    return jnp.maximum(x, 0.0) + jnp.log1p(jnp.exp(-jnp.abs(x)))


def _dot(a, b):
    return jnp.dot(a, b, preferred_element_type=F32)


def _dot_nt(a, b):
    return lax.dot_general(a, b, (((1,), (1,)), ((), ())), preferred_element_type=F32)


def _params(sem):
    return pltpu.CompilerParams(dimension_semantics=sem, vmem_limit_bytes=VMEM_LIMIT)


def _resident(shape):
    nd = len(shape)
    return pl.BlockSpec(shape, lambda *_: (0,) * nd, pipeline_mode=pl.Buffered(1))


FFN_TM = 1024
FFN_CW = 256


def _ffn_kernel(x_ref, nw_ref, wg_ref, wu_ref, wd_ref, fw_ref, o_ref, hn_s, a_s, *, final):
    hn_s[...] = _rms(x_ref[...], nw_ref[...]).astype(BF16)
    for c0 in range(0, D_FF, FFN_CW):
        hn = hn_s[...]
        g = _dot(hn, wg_ref[:, c0:c0 + FFN_CW])
        u = _dot(hn, wu_ref[:, c0:c0 + FFN_CW])
        a_s[:, c0:c0 + FFN_CW] = (_silu(g) * u).astype(BF16)
    y = x_ref[...] + 0.5 * _dot(a_s[...], wd_ref[...])
    if final:
        y = _rms(y, fw_ref[...])
    o_ref[...] = y


def _ffn(x, nw, wg, wu, wd, fw, final):
    n = x.shape[0]
    tm = min(FFN_TM, n)
    row = pl.BlockSpec((tm, D_MODEL), lambda i: (i, 0))
    return pl.pallas_call(
        functools.partial(_ffn_kernel, final=final),
        grid=(n // tm,),
        in_specs=[row, _resident((1, D_MODEL)), _resident((D_MODEL, D_FF)), _resident((D_MODEL, D_FF)),
                  _resident((D_FF, D_MODEL)), _resident((1, D_MODEL))],
        out_specs=row,
        out_shape=jax.ShapeDtypeStruct((n, D_MODEL), F32),
        scratch_shapes=[pltpu.VMEM((tm, D_MODEL), BF16), pltpu.VMEM((tm, D_FF), BF16)],
        compiler_params=_params(("parallel",)),
        name="ffn",
    )(x, nw, wg, wu, wd, fw)


INPROJ_TM = 512
INPROJ_GATE_ROWS = 256
INPROJ_CW = 256
INPROJ_RB = 64
_C_GOUT = 3 * KEY_WIDTH
_C_VAL = _C_GOUT + KEY_WIDTH
_C_GT = _C_VAL + D_MODEL
_C_GATES = _C_GT + D_MODEL
_C_END = _C_GATES + 2 * D_MODEL


def _split3(x):
    hi = x.astype(BF16)
    r1 = x - hi.astype(F32)
    mid = r1.astype(BF16)
    lo = (r1 - mid.astype(F32)).astype(BF16)
    return hi, mid, lo


def _inproj_kernel(x_ref, nw_ref, w_ref, wab_ref, cw_ref, alog_ref, dtb_ref,
                   q_ref, k_ref, v_ref, sg_ref, gb_ref, gct_ref, c_ref, gates_ref,
                   hn_s, xb, halo_s, *, tiles_per_seq):
    tm = x_ref.shape[0]
    cw = INPROJ_CW
    rb = INPROJ_RB

    @pl.when(pl.program_id(0) % tiles_per_seq == 0)
    def _():
        halo_s[...] = jnp.zeros_like(halo_s)

    hn_s[...] = _rms(x_ref[...], nw_ref[...]).astype(BF16)

    for group, out_ref in enumerate((q_ref, k_ref, v_ref)):
        g0 = group * KEY_WIDTH
        xc = xb.at[group]
        xc[0:SUBLANES, :] = halo_s[:, g0:g0 + KEY_WIDTH]
        xc[SUBLANES:SUBLANES + tm, :] = _dot(hn_s[...], w_ref[:, g0:g0 + KEY_WIDTH])
        halo_s[:, g0:g0 + KEY_WIDTH] = xc[tm:tm + SUBLANES, :]
    for group, out_ref in enumerate((q_ref, k_ref, v_ref)):
        g0 = group * KEY_WIDTH
        xc = xb.at[group]
        for c0 in range(0, KEY_WIDTH, cw):
            for r0 in range(0, tm, rb):
                xs = xc[r0:r0 + rb + SUBLANES, c0:c0 + cw]
                z = cw_ref[0:1, g0 + c0:g0 + c0 + cw] * xs
                for i in range(1, SHORT_CONV):
                    z = pltpu.roll(z, 1, axis=0) + cw_ref[i:i + 1, g0 + c0:g0 + c0 + cw] * xs
                y = _silu(z[SUBLANES:, :])
                for h0 in range(0, cw, HEAD_DIM):
                    yh = y[:, h0:h0 + HEAD_DIM]
                    if group == 0:
                        yh = yh * lax.rsqrt(jnp.sum(yh * yh, axis=-1, keepdims=True) + EPS) * (HEAD_DIM ** -0.5)
                    elif group == 1:
                        yh = yh * lax.rsqrt(jnp.sum(yh * yh, axis=-1, keepdims=True) + EPS)
                    out_ref[r0:r0 + rb, c0 + h0:c0 + h0 + HEAD_DIM] = yh.astype(out_ref.dtype)

    hn = hn_s[...]
    sg_ref[...] = _silu(_dot(hn, w_ref[:, _C_GOUT:_C_GOUT + KEY_WIDTH])).astype(sg_ref.dtype)

    ab = _dot(hn, wab_ref[...])
    g = -jnp.exp(alog_ref[...]) * _softplus(ab + dtb_ref[...])
    tg = min(INPROJ_GATE_ROWS, tm)
    r = lax.broadcasted_iota(jnp.int32, (tg, tg), 0)
    c = lax.broadcasted_iota(jnp.int32, (tg, tg), 1)
    tri = jnp.where((r // CHUNK == c // CHUNK) & (c <= r), 1.0, 0.0).astype(BF16)
    lane = lax.broadcasted_iota(jnp.int32, (tg, LANES), 1)
    for r0 in range(0, tm, tg):
        g_hi, g_mid, g_lo = _split3(g[r0:r0 + tg, :])
        gc = _dot(tri, g_hi) + _dot(tri, g_mid) + _dot(tri, g_lo)
        gb_ref[r0:r0 + tg, :] = jnp.where(lane < N_HEADS, gc, _sigmoid(ab[r0:r0 + tg, :]))
        gct = gc.T
        for p in range(tg // LANES):
            seg = gct[0:N_HEADS, p * LANES:(p + 1) * LANES]
            pair = (r0 + p * LANES) // CHUNK
            gct_ref[pair] = seg
            gct_ref[pair + 1] = pltpu.roll(seg, CHUNK, axis=1)

    val = _dot(hn, w_ref[:, _C_VAL:_C_VAL + D_MODEL])
    gt = _dot(hn, w_ref[:, _C_GT:_C_GT + D_MODEL])
    c_ref[...] = (val * _sigmoid(gt)).astype(c_ref.dtype)
    for c0 in range(0, 2 * D_MODEL, D_MODEL):
        gates_ref[:, c0:c0 + D_MODEL] = _dot(hn, w_ref[:, _C_GATES + c0:_C_GATES + c0 + D_MODEL]).astype(gates_ref.dtype)


def _inproj(x, nw, w_main, w_ab, conv_w, alog_pad, dtb_pad, seq):
    n = x.shape[0]
    tm = min(INPROJ_TM, seq)
    nch = tm // CHUNK
    row = lambda width: pl.BlockSpec((tm, width), lambda i: (i, 0))
    return pl.pallas_call(
        functools.partial(_inproj_kernel, tiles_per_seq=seq // tm),
        grid=(n // tm,),
        in_specs=[row(D_MODEL), _resident((1, D_MODEL)), _resident((D_MODEL, _C_END)),
                  _resident((D_MODEL, LANES)), _resident((SHORT_CONV, 3 * KEY_WIDTH)),
                  _resident((1, LANES)), _resident((1, LANES))],
        out_specs=[row(KEY_WIDTH), row(KEY_WIDTH), row(KEY_WIDTH), row(KEY_WIDTH), row(LANES),
                   pl.BlockSpec((nch, N_HEADS, LANES), lambda i: (i, 0, 0)),
                   row(D_MODEL), row(2 * D_MODEL)],
        out_shape=[
            jax.ShapeDtypeStruct((n, KEY_WIDTH), BF16),
            jax.ShapeDtypeStruct((n, KEY_WIDTH), BF16),
            jax.ShapeDtypeStruct((n, KEY_WIDTH), BF16),
            jax.ShapeDtypeStruct((n, KEY_WIDTH), BF16),
            jax.ShapeDtypeStruct((n, LANES), F32),
            jax.ShapeDtypeStruct((n // CHUNK, N_HEADS, LANES), F32),
            jax.ShapeDtypeStruct((n, D_MODEL), BF16),
            jax.ShapeDtypeStruct((n, 2 * D_MODEL), BF16),
        ],
        scratch_shapes=[pltpu.VMEM((tm, D_MODEL), BF16),
                        pltpu.VMEM((3, tm + SUBLANES, KEY_WIDTH), F32),
                        pltpu.VMEM((SUBLANES, 3 * KEY_WIDTH), F32)],
        compiler_params=_params(("arbitrary",)),
        name="inproj",
    )(x, nw, w_main, w_ab, conv_w, alog_pad, dtb_pad)


DN_TB = 512
DN_PA = 8
DN_G = 4


def _column(arr, lane):
    ids = lax.broadcasted_iota(jnp.int32, arr.shape, 1)
    return jnp.sum(jnp.where(ids == lane, arr, 0.0), axis=-1, keepdims=True)


CM_HALO = 32
CM_RB = 64


def _dwconv_rows(xbuf, w_ref, b_ref, ybuf, row0, nrows):
    base = CM_HALO - (DW_WIDTH - 1)
    span = CM_RB + CM_HALO
    for l0 in range(0, D_MODEL, LANES):
        for r0 in range(0, nrows, CM_RB):
            rs = pl.multiple_of(row0 + r0, CM_RB)
            xs = xbuf[pl.ds(rs, span), l0:l0 + LANES]
            acc = jnp.zeros((CM_RB, LANES), F32) + b_ref[:, l0:l0 + LANES]
            for phase in range(SUBLANES):
                taps = [i for i in range(DW_WIDTH) if (base + i) % SUBLANES == phase]
                if not taps:
                    continue
                xr = xs if phase == 0 else pltpu.roll(xs, span - phase, axis=0)
                for i in taps:
                    a0 = (base + i) - phase
                    acc = acc + w_ref[i:i + 1, l0:l0 + LANES] * xr[a0:a0 + CM_RB, :]
            ybuf[pl.ds(rs, CM_RB), l0:l0 + LANES] = acc


def _deltanet_kernel(q_ref, k_ref, v_ref, sg_ref, gb_ref, gct_ref, hn_ref, c_ref, cw_ref, cbias_ref, cn_ref,
                     o_ref, cb_ref, state_s, lhs_s, bm_s, ou_s, eg_s, xbuf, ybuf):
    tb = q_ref.shape[0]
    nch = tb // CHUNK

    @pl.when(pl.program_id(1) == 0)
    def _():
        state_s[...] = jnp.zeros_like(state_s)
        xbuf[0:CM_HALO, :] = jnp.zeros((CM_HALO, D_MODEL), F32)

    xbuf[CM_HALO:CM_HALO + tb, :] = c_ref[...].astype(F32)

    hnw = hn_ref[...]
    gw = DN_G * CHUNK
    gl = DN_G * HEAD_DIM
    prow = lax.broadcasted_iota(jnp.int32, (CHUNK, gw), 0)
    plane = lax.broadcasted_iota(jnp.int32, (CHUNK, gw), 1)
    pcol = plane & (CHUNK - 1)
    pgrp = [plane // CHUNK == j for j in range(DN_G)]
    incl = prow >= pcol
    strict = prow > pcol
    eye = jnp.where(prow == pcol, 1.0, 0.0)
    fgrp = [lax.broadcasted_iota(jnp.int32, (CHUNK, gl), 1) // HEAD_DIM == j for j in range(DN_G)]
    lane_lo = lax.broadcasted_iota(jnp.int32, (1, LANES), 1) < CHUNK

    def head_lanes(h):
        return slice(h * HEAD_DIM, (h + 1) * HEAD_DIM)

    def per_head(cols):
        out = cols[DN_G - 1]
        for j in range(DN_G - 2, -1, -1):
            out = jnp.where(pgrp[j], cols[j], out)
        return out

    def block_diag(x, masks):
        return jnp.concatenate([jnp.where(m, x, jnp.zeros_like(x)) for m in masks], axis=0)

    def phase_a(it, carry):
        _dwconv_rows(xbuf, cw_ref, cbias_ref, ybuf, it * (DN_PA * CHUNK), DN_PA * CHUNK)
        probs = [(it * DN_PA + cc, g) for cc in range(DN_PA) for g in range(N_HEADS // DN_G)]
        pre = []
        for ch, g in probs:
            rows = pl.ds(pl.multiple_of(ch * CHUNK, CHUNK), CHUNK)
            gb_blk = gb_ref[rows, :]
            lanes = slice(g * gl, (g + 1) * gl)
            q4 = q_ref[rows, lanes]
            k4 = k_ref[rows, lanes]
            k4f = k4.astype(F32)
            heads = [g * DN_G + j for j in range(DN_G)]
            beta = [_column(gb_blk, N_HEADS + h) for h in heads]
            gcc = [_column(gb_blk, h) for h in heads]
            g_last = [c[CHUNK - 1:CHUNK, :] for c in gcc]
            for h, gl_h in zip(heads, g_last):
                eg_s[ch * N_HEADS + h] = jnp.broadcast_to(jnp.exp(gl_h), (SUBLANES, LANES))
            wide = lambda cols: jnp.concatenate([jnp.broadcast_to(c, (CHUNK, HEAD_DIM)) for c in cols], axis=1)
            beta_x = wide(beta)
            egc_x = wide([jnp.exp(c) for c in gcc])
            ekd_x = wide([jnp.exp(gl_h - c) for gl_h, c in zip(g_last, gcc)])
            gct = gct_ref[ch]
            gct_hi = pltpu.roll(gct, CHUNK, axis=1)
            pair = lambda a: jnp.where(lane_lo, gct[a:a + 1, :], gct_hi[a + 1:a + 2, :])
            gcr = jnp.concatenate([pair(heads[0]), pair(heads[2])], axis=1)
            e_incl = jnp.exp(jnp.where(incl, per_head(gcc) - gcr, -jnp.inf))
            kb = k4f * beta_x
            vb16 = (v_ref[rows, lanes].astype(F32) * beta_x).astype(BF16)
            kbg16 = (kb * egc_x).astype(BF16)
            rhs = jnp.concatenate(
                [jnp.concatenate([vb16[:, head_lanes(j)], kbg16[:, head_lanes(j)]], axis=1) for j in range(DN_G)],
                axis=0)
            pre.append(dict(
                lhs_kq=jnp.concatenate([kb.astype(BF16), q4], axis=0),
                kbd=block_diag(k4, fgrp), e_incl=e_incl, rhs=rhs,
                qd=q4.astype(F32) * egc_x, kdbd=block_diag((k4f * ekd_x).astype(BF16), fgrp)))

        kqs = [_dot_nt(d["lhs_kq"], d["kbd"]) for d in pre]
        ms = [kq[0:CHUNK, :] * jnp.where(strict, d["e_incl"], 0.0) for kq, d in zip(kqs, pre)]
        attns = [(kq[CHUNK:2 * CHUNK, :] * d["e_incl"]).astype(BF16) for kq, d in zip(kqs, pre)]

        p16s = [(-m).astype(BF16) for m in ms]
        tinvs = [eye - m for m in ms]
        pbds = [block_diag(p16, pgrp) for p16 in p16s]
        for _ in range(5):
            p16s = [_dot(p16, pbd).astype(BF16) for p16, pbd in zip(p16s, pbds)]
            pbds = [block_diag(p16, pgrp) for p16 in p16s]
            tinvs = [tv + _dot(tv.astype(BF16), pbd) for tv, pbd in zip(tinvs, pbds)]

        uws = [_dot(block_diag(tv.astype(BF16), pgrp), d["rhs"]).astype(BF16)
               for tv, d in zip(tinvs, pre)]
        auws = [_dot(block_diag(a, pgrp), uw) for a, uw in zip(attns, uws)]
        kuws = [lax.dot_general(d["kdbd"], uw, (((0,), (0,)), ((), ())), preferred_element_type=F32)
                for d, uw in zip(pre, uws)]
        for (ch, g), d, kuw, auw in zip(probs, pre, kuws, auws):
            rows = pl.ds(pl.multiple_of(ch * CHUNK, CHUNK), CHUNK)
            for j in range(DN_G):
                h = g * DN_G + j
                idx = ch * N_HEADS + h
                au = auw[j * CHUNK:(j + 1) * CHUNK, :]
                ku = kuw[j * HEAD_DIM:(j + 1) * HEAD_DIM, :]
                qeff = d["qd"][:, head_lanes(j)] - au[:, HEAD_DIM:2 * HEAD_DIM]
                lhs_s[idx, 0:CHUNK, :] = qeff.astype(BF16)
                lhs_s[idx, CHUNK:CHUNK + HEAD_DIM, :] = ku[:, HEAD_DIM:2 * HEAD_DIM].astype(BF16)
                bm_s[idx] = ku[:, 0:HEAD_DIM]
                ou_s[rows, head_lanes(h)] = au[:, 0:HEAD_DIM]
        return carry

    lax.fori_loop(0, nch // DN_PA, phase_a, 0)

    def phase_b(ch, carry):
        rows = pl.ds(pl.multiple_of(ch * CHUNK, CHUNK), CHUNK)
        res = [_dot(lhs_s[ch * N_HEADS + h], state_s[h].astype(BF16)) for h in range(N_HEADS)]
        for h in range(N_HEADS):
            idx = ch * N_HEADS + h
            state_s[h] = (state_s[h] * eg_s[idx][0:1, :] - res[h][CHUNK:CHUNK + HEAD_DIM, :] + bm_s[idx])
            o = res[h][0:CHUNK, :] + ou_s[rows, head_lanes(h)]
            o = _rms(o, hnw) * sg_ref[rows, head_lanes(h)].astype(F32)
            o_ref[rows, head_lanes(h)] = o.astype(o_ref.dtype)
        cb_ref[rows, :] = _silu(_rms(ybuf[rows, :], cn_ref[...])).astype(cb_ref.dtype)
        return carry

    lax.fori_loop(0, nch, phase_b, 0)
    xbuf[0:CM_HALO, :] = xbuf[tb:tb + CM_HALO, :]


def _deltanet(q, k, v, sg, gb, gct, head_norm, c, cw, cbias, cn, bsz, seq):
    n = q.shape[0]
    tb = min(DN_TB, seq)
    nt = seq // tb
    nch = tb // CHUNK
    assert seq % tb == 0 and nch % DN_PA == 0 and N_HEADS % DN_G == 0
    row = lambda width: pl.BlockSpec((tb, width), lambda b, t: (b * nt + t, 0))
    return pl.pallas_call(
        _deltanet_kernel,
        grid=(bsz, nt),
        in_specs=[row(KEY_WIDTH), row(KEY_WIDTH), row(KEY_WIDTH), row(KEY_WIDTH), row(LANES),
                  pl.BlockSpec((nch, N_HEADS, LANES), lambda b, t: (b * nt + t, 0, 0)),
                  _resident((1, HEAD_DIM)), row(D_MODEL), _resident((DW_WIDTH, D_MODEL)),
                  _resident((1, D_MODEL)), _resident((1, D_MODEL))],
        out_specs=[row(KEY_WIDTH), row(D_MODEL)],
        out_shape=[jax.ShapeDtypeStruct((n, KEY_WIDTH), BF16), jax.ShapeDtypeStruct((n, D_MODEL), BF16)],
        scratch_shapes=[
            pltpu.VMEM((N_HEADS, HEAD_DIM, HEAD_DIM), F32),
            pltpu.VMEM((nch * N_HEADS, CHUNK + HEAD_DIM, HEAD_DIM), BF16),
            pltpu.VMEM((nch * N_HEADS, HEAD_DIM, HEAD_DIM), F32),
            pltpu.VMEM((tb, KEY_WIDTH), F32),
            pltpu.VMEM((nch * N_HEADS, SUBLANES, LANES), F32),
            pltpu.VMEM((tb + CM_HALO, D_MODEL), F32),
            pltpu.VMEM((tb, D_MODEL), F32),
        ],
        compiler_params=_params(("parallel", "arbitrary")),
        name="deltanet",
    )(q, k, v, sg, gb, gct, head_norm, c, cw, cbias, cn)


OUT_TM = 512


def _outproj_kernel(x_ref, oa_ref, cb_ref, gates_ref, wa_ref, wb_ref, wo_ref, o_ref):
    ya = _dot(oa_ref[...], wa_ref[...])
    yb = _dot(cb_ref[...], wb_ref[...])
    merged = (_sigmoid(gates_ref[:, 0:D_MODEL].astype(F32)) * ya
              + _sigmoid(gates_ref[:, D_MODEL:2 * D_MODEL].astype(F32)) * yb)
    o_ref[...] = x_ref[...] + _dot(merged.astype(BF16), wo_ref[...])


def _outproj(x, oa, cb, gates, wa, wb, wo):
    n = x.shape[0]
    tm = min(OUT_TM, n)
    row = lambda width: pl.BlockSpec((tm, width), lambda i: (i, 0))
    wspec = pl.BlockSpec((D_MODEL, D_MODEL), lambda i: (0, 0))
    return pl.pallas_call(
        _outproj_kernel,
        grid=(n // tm,),
        in_specs=[row(D_MODEL), row(KEY_WIDTH), row(D_MODEL), row(2 * D_MODEL), wspec, wspec, wspec],
        out_specs=row(D_MODEL),
        out_shape=jax.ShapeDtypeStruct((n, D_MODEL), F32),
        compiler_params=_params(("parallel",)),
        name="outproj",
    )(x, oa, cb, gates, wa, wb, wo)


def _pad_lanes(v, offset=0):
    return jnp.zeros((1, LANES), F32).at[0, offset:offset + v.shape[0]].set(v.astype(F32))


def kernel(x, ffn1_norm, ffn1_w_gate, ffn1_w_up, ffn1_w_down, mix_norm, w_in, qkv_conv_w, a_log, dt_bias, head_norm, w_a_out, dw_conv_w, dw_conv_b, conv_norm, w_b_out, w_o, ffn2_norm, ffn2_w_gate, ffn2_w_up, ffn2_w_down, final_norm):
    bsz, seq, _ = x.shape
    depth = w_in.shape[0]
    h = x.reshape(bsz * seq, D_MODEL)
    fw = final_norm.reshape(1, D_MODEL)
    n_ab = 2 * N_HEADS
    c_ab = 4 * KEY_WIDTH
    for l in range(depth):
        h = _ffn(h, ffn1_norm[l].reshape(1, -1), ffn1_w_gate[l].astype(BF16), ffn1_w_up[l].astype(BF16),
                 ffn1_w_down[l].astype(BF16), fw, final=False)
        w_main = jnp.concatenate([w_in[l][:, :c_ab], w_in[l][:, c_ab + n_ab:]], axis=1).astype(BF16)
        w_ab = jnp.pad(w_in[l][:, c_ab:c_ab + n_ab], ((0, 0), (0, LANES - n_ab))).astype(BF16)
        q, k, v, sg, gb, gct, c, gates = _inproj(
            h, mix_norm[l].reshape(1, -1), w_main, w_ab, qkv_conv_w[l],
            _pad_lanes(a_log[l]), _pad_lanes(dt_bias[l]), seq)
        oa, cb = _deltanet(q, k, v, sg, gb, gct, head_norm[l].reshape(1, -1), c, dw_conv_w[l],
                           dw_conv_b[l].reshape(1, -1), conv_norm[l].reshape(1, -1), bsz, seq)
        h = _outproj(h, oa, cb, gates, w_a_out[l].astype(BF16), w_b_out[l].astype(BF16), w_o[l].astype(BF16))
        h = _ffn(h, ffn2_norm[l].reshape(1, -1), ffn2_w_gate[l].astype(BF16), ffn2_w_up[l].astype(BF16),
                 ffn2_w_down[l].astype(BF16), fw, final=(l == depth - 1))
    return h.reshape(bsz, seq, D_MODEL)
```

```python
import functools

import jax
import jax.numpy as jnp
from jax import lax
from jax.experimental import pallas as pl
from jax.experimental.pallas import tpu as pltpu

D_MODEL = 1024
N_HEADS = 8
HEAD_DIM = 128
KEY_WIDTH = N_HEADS * HEAD_DIM
CHUNK = 64
SHORT_CONV = 4
DW_WIDTH = 31
D_FF = 2816
EPS = 1e-6

LANES = 128
SUBLANES = 8
VMEM_LIMIT = 56 * 1024 * 1024

F32 = jnp.float32
BF16 = jnp.bfloat16


def _rms(x, w):
    return x * lax.rsqrt(jnp.mean(x * x, axis=-1, keepdims=True) + EPS) * w


def _sigmoid(x):
    return 1.0 / (1.0 + jnp.exp(-x))


def _silu(x):
    return x * _sigmoid(x)


def _softplus(x):
    return jnp.maximum(x, 0.0) + jnp.log1p(jnp.exp(-jnp.abs(x)))


def _dot(a, b):
    return jnp.dot(a, b, preferred_element_type=F32)


def _dot_nt(a, b):
    return lax.dot_general(a, b, (((1,), (1,)), ((), ())), preferred_element_type=F32)


def _params(sem):
    return pltpu.CompilerParams(dimension_semantics=sem, vmem_limit_bytes=VMEM_LIMIT)


def _resident(shape):
    nd = len(shape)
    return pl.BlockSpec(shape, lambda *_: (0,) * nd, pipeline_mode=pl.Buffered(1))


FFN_TM = 1024
FFN_CW = 256


def _ffn_kernel(x_ref, nw_ref, wg_ref, wu_ref, wd_ref, fw_ref, o_ref, hn_s, a_s, *, final):
    hn_s[...] = _rms(x_ref[...], nw_ref[...]).astype(BF16)
    for c0 in range(0, D_FF, FFN_CW):
        hn = hn_s[...]
        g = _dot(hn, wg_ref[:, c0:c0 + FFN_CW])
        u = _dot(hn, wu_ref[:, c0:c0 + FFN_CW])
        a_s[:, c0:c0 + FFN_CW] = (_silu(g) * u).astype(BF16)
    y = x_ref[...] + 0.5 * _dot(a_s[...], wd_ref[...])
    if final:
        y = _rms(y, fw_ref[...])
    o_ref[...] = y


def _ffn(x, nw, wg, wu, wd, fw, final):
    n = x.shape[0]
    tm = min(FFN_TM, n)
    row = pl.BlockSpec((tm, D_MODEL), lambda i: (i, 0))
    return pl.pallas_call(
        functools.partial(_ffn_kernel, final=final),
        grid=(n // tm,),
        in_specs=[row, _resident((1, D_MODEL)), _resident((D_MODEL, D_FF)), _resident((D_MODEL, D_FF)),
                  _resident((D_FF, D_MODEL)), _resident((1, D_MODEL))],
        out_specs=row,
        out_shape=jax.ShapeDtypeStruct((n, D_MODEL), F32),
        scratch_shapes=[pltpu.VMEM((tm, D_MODEL), BF16), pltpu.VMEM((tm, D_FF), BF16)],
        compiler_params=_params(("parallel",)),
        name="ffn",
    )(x, nw, wg, wu, wd, fw)


INPROJ_TM = 512
INPROJ_GATE_ROWS = 256
INPROJ_CW = 256
INPROJ_RB = 128
_C_GOUT = 3 * KEY_WIDTH
_C_VAL = _C_GOUT + KEY_WIDTH
_C_GT = _C_VAL + D_MODEL
_C_GATES = _C_GT + D_MODEL
_C_END = _C_GATES + 2 * D_MODEL


def _split3(x):
    hi = x.astype(BF16)
    r1 = x - hi.astype(F32)
    mid = r1.astype(BF16)
    lo = (r1 - mid.astype(F32)).astype(BF16)
    return hi, mid, lo


def _inproj_kernel(x_ref, nw_ref, w_ref, wab_ref, cw_ref, alog_ref, dtb_ref,
                   q_ref, k_ref, v_ref, sg_ref, gb_ref, gct_ref, c_ref, gates_ref,
                   hn_s, xb, halo_s, *, tiles_per_seq):
    tm = x_ref.shape[0]
    cw = INPROJ_CW
    rb = INPROJ_RB

    @pl.when(pl.program_id(0) % tiles_per_seq == 0)
    def _():
        halo_s[...] = jnp.zeros_like(halo_s)

    hn_s[...] = _rms(x_ref[...], nw_ref[...]).astype(BF16)

    for group, out_ref in enumerate((q_ref, k_ref, v_ref)):
        g0 = group * KEY_WIDTH
        xc = xb.at[group]
        xc[0:SUBLANES, :] = halo_s[:, g0:g0 + KEY_WIDTH]
        xc[SUBLANES:SUBLANES + tm, :] = _dot(hn_s[...], w_ref[:, g0:g0 + KEY_WIDTH])
        halo_s[:, g0:g0 + KEY_WIDTH] = xc[tm:tm + SUBLANES, :]
    for group, out_ref in enumerate((q_ref, k_ref, v_ref)):
        g0 = group * KEY_WIDTH
        xc = xb.at[group]
        for c0 in range(0, KEY_WIDTH, cw):
            for r0 in range(0, tm, rb):
                xs = xc[r0:r0 + rb + SUBLANES, c0:c0 + cw]
                z = cw_ref[0:1, g0 + c0:g0 + c0 + cw] * xs
                for i in range(1, SHORT_CONV):
                    z = pltpu.roll(z, 1, axis=0) + cw_ref[i:i + 1, g0 + c0:g0 + c0 + cw] * xs
                y = _silu(z[SUBLANES:, :])
                for h0 in range(0, cw, HEAD_DIM):
                    yh = y[:, h0:h0 + HEAD_DIM]
                    if group == 0:
                        yh = yh * lax.rsqrt(jnp.sum(yh * yh, axis=-1, keepdims=True) + EPS) * (HEAD_DIM ** -0.5)
                    elif group == 1:
                        yh = yh * lax.rsqrt(jnp.sum(yh * yh, axis=-1, keepdims=True) + EPS)
                    out_ref[r0:r0 + rb, c0 + h0:c0 + h0 + HEAD_DIM] = yh.astype(out_ref.dtype)

    hn = hn_s[...]
    sg_ref[...] = _silu(_dot(hn, w_ref[:, _C_GOUT:_C_GOUT + KEY_WIDTH])).astype(sg_ref.dtype)

    ab = _dot(hn, wab_ref[...])
    g = -jnp.exp(alog_ref[...]) * _softplus(ab + dtb_ref[...])
    tg = min(INPROJ_GATE_ROWS, tm)
    r = lax.broadcasted_iota(jnp.int32, (tg, tg), 0)
    c = lax.broadcasted_iota(jnp.int32, (tg, tg), 1)
    tri = jnp.where((r // CHUNK == c // CHUNK) & (c <= r), 1.0, 0.0).astype(BF16)
    lane = lax.broadcasted_iota(jnp.int32, (tg, LANES), 1)
    for r0 in range(0, tm, tg):
        g_hi, g_mid, g_lo = _split3(g[r0:r0 + tg, :])
        gc = _dot(tri, g_hi) + _dot(tri, g_mid) + _dot(tri, g_lo)
        gb_ref[r0:r0 + tg, :] = jnp.where(lane < N_HEADS, gc, _sigmoid(ab[r0:r0 + tg, :]))
        gct = gc.T
        for p in range(tg // LANES):
            seg = gct[0:N_HEADS, p * LANES:(p + 1) * LANES]
            pair = (r0 + p * LANES) // CHUNK
            gct_ref[pair] = seg
            gct_ref[pair + 1] = pltpu.roll(seg, CHUNK, axis=1)

    val = _dot(hn, w_ref[:, _C_VAL:_C_VAL + D_MODEL])
    gt = _dot(hn, w_ref[:, _C_GT:_C_GT + D_MODEL])
    c_ref[...] = (val * _sigmoid(gt)).astype(c_ref.dtype)
    for c0 in range(0, 2 * D_MODEL, D_MODEL):
        gates_ref[:, c0:c0 + D_MODEL] = _dot(hn, w_ref[:, _C_GATES + c0:_C_GATES + c0 + D_MODEL]).astype(gates_ref.dtype)


def _inproj(x, nw, w_main, w_ab, conv_w, alog_pad, dtb_pad, seq):
    n = x.shape[0]
    tm = min(INPROJ_TM, seq)
    nch = tm // CHUNK
    row = lambda width: pl.BlockSpec((tm, width), lambda i: (i, 0))
    return pl.pallas_call(
        functools.partial(_inproj_kernel, tiles_per_seq=seq // tm),
        grid=(n // tm,),
        in_specs=[row(D_MODEL), _resident((1, D_MODEL)), _resident((D_MODEL, _C_END)),
                  _resident((D_MODEL, LANES)), _resident((SHORT_CONV, 3 * KEY_WIDTH)),
                  _resident((1, LANES)), _resident((1, LANES))],
        out_specs=[row(KEY_WIDTH), row(KEY_WIDTH), row(KEY_WIDTH), row(KEY_WIDTH), row(LANES),
                   pl.BlockSpec((nch, N_HEADS, LANES), lambda i: (i, 0, 0)),
                   row(D_MODEL), row(2 * D_MODEL)],
        out_shape=[
            jax.ShapeDtypeStruct((n, KEY_WIDTH), BF16),
            jax.ShapeDtypeStruct((n, KEY_WIDTH), BF16),
            jax.ShapeDtypeStruct((n, KEY_WIDTH), BF16),
            jax.ShapeDtypeStruct((n, KEY_WIDTH), BF16),
            jax.ShapeDtypeStruct((n, LANES), F32),
            jax.ShapeDtypeStruct((n // CHUNK, N_HEADS, LANES), F32),
            jax.ShapeDtypeStruct((n, D_MODEL), BF16),
            jax.ShapeDtypeStruct((n, 2 * D_MODEL), BF16),
        ],
        scratch_shapes=[pltpu.VMEM((tm, D_MODEL), BF16),
                        pltpu.VMEM((3, tm + SUBLANES, KEY_WIDTH), F32),
                        pltpu.VMEM((SUBLANES, 3 * KEY_WIDTH), F32)],
        compiler_params=_params(("arbitrary",)),
        name="inproj",
    )(x, nw, w_main, w_ab, conv_w, alog_pad, dtb_pad)


DN_TB = 512
DN_PA = 8
DN_G = 4


def _column(arr, lane):
    ids = lax.broadcasted_iota(jnp.int32, arr.shape, 1)
    return jnp.sum(jnp.where(ids == lane, arr, 0.0), axis=-1, keepdims=True)


CM_HALO = 32
CM_RB = 64


def _dwconv_rows(xbuf, w_ref, b_ref, ybuf, row0, nrows):
    base = CM_HALO - (DW_WIDTH - 1)
    span = CM_RB + CM_HALO
    for l0 in range(0, D_MODEL, LANES):
        for r0 in range(0, nrows, CM_RB):
            rs = pl.multiple_of(row0 + r0, CM_RB)
            xs = xbuf[pl.ds(rs, span), l0:l0 + LANES]
            acc = jnp.zeros((CM_RB, LANES), F32) + b_ref[:, l0:l0 + LANES]
            for phase in range(SUBLANES):
                taps = [i for i in range(DW_WIDTH) if (base + i) % SUBLANES == phase]
                if not taps:
                    continue
                xr = xs if phase == 0 else pltpu.roll(xs, span - phase, axis=0)
                for i in taps:
                    a0 = (base + i) - phase
                    acc = acc + w_ref[i:i + 1, l0:l0 + LANES] * xr[a0:a0 + CM_RB, :]
            ybuf[pl.ds(rs, CM_RB), l0:l0 + LANES] = acc


def _deltanet_kernel(q_ref, k_ref, v_ref, sg_ref, gb_ref, gct_ref, hn_ref, c_ref, cw_ref, cbias_ref, cn_ref,
                     o_ref, cb_ref, state_s, lhs_s, bm_s, ou_s, eg_s, xbuf, ybuf):
    tb = q_ref.shape[0]
    nch = tb // CHUNK

    @pl.when(pl.program_id(1) == 0)
    def _():
        state_s[...] = jnp.zeros_like(state_s)
        xbuf[0:CM_HALO, :] = jnp.zeros((CM_HALO, D_MODEL), F32)

    xbuf[CM_HALO:CM_HALO + tb, :] = c_ref[...].astype(F32)

    hnw = hn_ref[...]
    gw = DN_G * CHUNK
    gl = DN_G * HEAD_DIM
    prow = lax.broadcasted_iota(jnp.int32, (CHUNK, gw), 0)
    plane = lax.broadcasted_iota(jnp.int32, (CHUNK, gw), 1)
    pcol = plane & (CHUNK - 1)
    pgrp = [plane // CHUNK == j for j in range(DN_G)]
    incl = prow >= pcol
    strict = prow > pcol
    eye = jnp.where(prow == pcol, 1.0, 0.0)
    fgrp = [lax.broadcasted_iota(jnp.int32, (CHUNK, gl), 1) // HEAD_DIM == j for j in range(DN_G)]
    lane_lo = lax.broadcasted_iota(jnp.int32, (1, LANES), 1) < CHUNK

    def head_lanes(h):
        return slice(h * HEAD_DIM, (h + 1) * HEAD_DIM)

    def per_head(cols):
        out = cols[DN_G - 1]
        for j in range(DN_G - 2, -1, -1):
            out = jnp.where(pgrp[j], cols[j], out)
        return out

    def block_diag(x, masks):
        return jnp.concatenate([jnp.where(m, x, jnp.zeros_like(x)) for m in masks], axis=0)

    def phase_a(it, carry):
        _dwconv_rows(xbuf, cw_ref, cbias_ref, ybuf, it * (DN_PA * CHUNK), DN_PA * CHUNK)
        probs = [(it * DN_PA + cc, g) for cc in range(DN_PA) for g in range(N_HEADS // DN_G)]
        pre = []
        for ch, g in probs:
            rows = pl.ds(pl.multiple_of(ch * CHUNK, CHUNK), CHUNK)
            gb_blk = gb_ref[rows, :]
            lanes = slice(g * gl, (g + 1) * gl)
            q4 = q_ref[rows, lanes]
            k4 = k_ref[rows, lanes]
            k4f = k4.astype(F32)
            heads = [g * DN_G + j for j in range(DN_G)]
            beta = [_column(gb_blk, N_HEADS + h) for h in heads]
            gcc = [_column(gb_blk, h) for h in heads]
            g_last = [c[CHUNK - 1:CHUNK, :] for c in gcc]
            for h, gl_h in zip(heads, g_last):
                eg_s[ch * N_HEADS + h] = jnp.broadcast_to(jnp.exp(gl_h), (SUBLANES, LANES))
            wide = lambda cols: jnp.concatenate([jnp.broadcast_to(c, (CHUNK, HEAD_DIM)) for c in cols], axis=1)
            beta_x = wide(beta)
            egc_x = wide([jnp.exp(c) for c in gcc])
            ekd_x = wide([jnp.exp(gl_h - c) for gl_h, c in zip(g_last, gcc)])
            gct = gct_ref[ch]
            gct_hi = pltpu.roll(gct, CHUNK, axis=1)
            pair = lambda a: jnp.where(lane_lo, gct[a:a + 1, :], gct_hi[a + 1:a + 2, :])
            gcr = jnp.concatenate([pair(heads[0]), pair(heads[2])], axis=1)
            e_incl = jnp.exp(jnp.where(incl, per_head(gcc) - gcr, -jnp.inf))
            kb = k4f * beta_x
            vb16 = (v_ref[rows, lanes].astype(F32) * beta_x).astype(BF16)
            kbg16 = (kb * egc_x).astype(BF16)
            rhs = jnp.concatenate(
                [jnp.concatenate([vb16[:, head_lanes(j)], kbg16[:, head_lanes(j)]], axis=1) for j in range(DN_G)],
                axis=0)
            pre.append(dict(
                lhs_kq=jnp.concatenate([kb.astype(BF16), q4], axis=0),
                kbd=block_diag(k4, fgrp), e_incl=e_incl, rhs=rhs,
                qd=q4.astype(F32) * egc_x, kdbd=block_diag((k4f * ekd_x).astype(BF16), fgrp)))

        kqs = [_dot_nt(d["lhs_kq"], d["kbd"]) for d in pre]
        ms = [kq[0:CHUNK, :] * jnp.where(strict, d["e_incl"], 0.0) for kq, d in zip(kqs, pre)]
        attns = [(kq[CHUNK:2 * CHUNK, :] * d["e_incl"]).astype(BF16) for kq, d in zip(kqs, pre)]

        p16s = [(-m).astype(BF16) for m in ms]
        tinvs = [eye - m for m in ms]
        pbds = [block_diag(p16, pgrp) for p16 in p16s]
        for _ in range(5):
            p16s = [_dot(p16, pbd).astype(BF16) for p16, pbd in zip(p16s, pbds)]
            pbds = [block_diag(p16, pgrp) for p16 in p16s]
            tinvs = [tv + _dot(tv.astype(BF16), pbd) for tv, pbd in zip(tinvs, pbds)]

        uws = [_dot(block_diag(tv.astype(BF16), pgrp), d["rhs"]).astype(BF16)
               for tv, d in zip(tinvs, pre)]
        auws = [_dot(block_diag(a, pgrp), uw) for a, uw in zip(attns, uws)]
        kuws = [lax.dot_general(d["kdbd"], uw, (((0,), (0,)), ((), ())), preferred_element_type=F32)
                for d, uw in zip(pre, uws)]
        for (ch, g), d, kuw, auw in zip(probs, pre, kuws, auws):
            rows = pl.ds(pl.multiple_of(ch * CHUNK, CHUNK), CHUNK)
            for j in range(DN_G):
                h = g * DN_G + j
                idx = ch * N_HEADS + h
                au = auw[j * CHUNK:(j + 1) * CHUNK, :]
                ku = kuw[j * HEAD_DIM:(j + 1) * HEAD_DIM, :]
                qeff = d["qd"][:, head_lanes(j)] - au[:, HEAD_DIM:2 * HEAD_DIM]
                lhs_s[idx, 0:CHUNK, :] = qeff.astype(BF16)
                lhs_s[idx, CHUNK:CHUNK + HEAD_DIM, :] = ku[:, HEAD_DIM:2 * HEAD_DIM].astype(BF16)
                bm_s[idx] = ku[:, 0:HEAD_DIM]
                ou_s[rows, head_lanes(h)] = au[:, 0:HEAD_DIM]
        return carry

    lax.fori_loop(0, nch // DN_PA, phase_a, 0)

    def phase_b(ch, carry):
        rows = pl.ds(pl.multiple_of(ch * CHUNK, CHUNK), CHUNK)
        res = [_dot(lhs_s[ch * N_HEADS + h], state_s[h].astype(BF16)) for h in range(N_HEADS)]
        for h in range(N_HEADS):
            idx = ch * N_HEADS + h
            state_s[h] = (state_s[h] * eg_s[idx][0:1, :] - res[h][CHUNK:CHUNK + HEAD_DIM, :] + bm_s[idx])
            o = res[h][0:CHUNK, :] + ou_s[rows, head_lanes(h)]
            o = _rms(o, hnw) * sg_ref[rows, head_lanes(h)].astype(F32)
            o_ref[rows, head_lanes(h)] = o.astype(o_ref.dtype)
        cb_ref[rows, :] = _silu(_rms(ybuf[rows, :], cn_ref[...])).astype(cb_ref.dtype)
        return carry

    lax.fori_loop(0, nch, phase_b, 0)
    xbuf[0:CM_HALO, :] = xbuf[tb:tb + CM_HALO, :]


def _deltanet(q, k, v, sg, gb, gct, head_norm, c, cw, cbias, cn, bsz, seq):
    n = q.shape[0]
    tb = min(DN_TB, seq)
    nt = seq // tb
    nch = tb // CHUNK
    assert seq % tb == 0 and nch % DN_PA == 0 and N_HEADS % DN_G == 0
    row = lambda width: pl.BlockSpec((tb, width), lambda b, t: (b * nt + t, 0))
    return pl.pallas_call(
        _deltanet_kernel,
        grid=(bsz, nt),
        in_specs=[row(KEY_WIDTH), row(KEY_WIDTH), row(KEY_WIDTH), row(KEY_WIDTH), row(LANES),
                  pl.BlockSpec((nch, N_HEADS, LANES), lambda b, t: (b * nt + t, 0, 0)),
                  _resident((1, HEAD_DIM)), row(D_MODEL), _resident((DW_WIDTH, D_MODEL)),
                  _resident((1, D_MODEL)), _resident((1, D_MODEL))],
        out_specs=[row(KEY_WIDTH), row(D_MODEL)],
        out_shape=[jax.ShapeDtypeStruct((n, KEY_WIDTH), BF16), jax.ShapeDtypeStruct((n, D_MODEL), BF16)],
        scratch_shapes=[
            pltpu.VMEM((N_HEADS, HEAD_DIM, HEAD_DIM), F32),
            pltpu.VMEM((nch * N_HEADS, CHUNK + HEAD_DIM, HEAD_DIM), BF16),
            pltpu.VMEM((nch * N_HEADS, HEAD_DIM, HEAD_DIM), F32),
            pltpu.VMEM((tb, KEY_WIDTH), F32),
            pltpu.VMEM((nch * N_HEADS, SUBLANES, LANES), F32),
            pltpu.VMEM((tb + CM_HALO, D_MODEL), F32),
            pltpu.VMEM((tb, D_MODEL), F32),
        ],
        compiler_params=_params(("parallel", "arbitrary")),
        name="deltanet",
    )(q, k, v, sg, gb, gct, head_norm, c, cw, cbias, cn)


OUT_TM = 1024


def _outproj_kernel(x_ref, oa_ref, cb_ref, gates_ref, wa_ref, wb_ref, wo_ref, o_ref):
    ya = _dot(oa_ref[...], wa_ref[...])
    yb = _dot(cb_ref[...], wb_ref[...])
    merged = (_sigmoid(gates_ref[:, 0:D_MODEL].astype(F32)) * ya
              + _sigmoid(gates_ref[:, D_MODEL:2 * D_MODEL].astype(F32)) * yb)
    o_ref[...] = x_ref[...] + _dot(merged.astype(BF16), wo_ref[...])


def _outproj(x, oa, cb, gates, wa, wb, wo):
    n = x.shape[0]
    tm = min(OUT_TM, n)
    row = lambda width: pl.BlockSpec((tm, width), lambda i: (i, 0))
    wspec = _resident((D_MODEL, D_MODEL))
    return pl.pallas_call(
        _outproj_kernel,
        grid=(n // tm,),
        in_specs=[row(D_MODEL), row(KEY_WIDTH), row(D_MODEL), row(2 * D_MODEL), wspec, wspec, wspec],
        out_specs=row(D_MODEL),
        out_shape=jax.ShapeDtypeStruct((n, D_MODEL), F32),
        compiler_params=_params(("parallel",)),
        name="outproj",
    )(x, oa, cb, gates, wa, wb, wo)


def _pad_lanes(v):
    return jnp.zeros((1, LANES), F32).at[0, 0:v.shape[0]].set(v.astype(F32))


def kernel(x, ffn1_norm, ffn1_w_gate, ffn1_w_up, ffn1_w_down, mix_norm, w_in, qkv_conv_w, a_log, dt_bias, head_norm, w_a_out, dw_conv_w, dw_conv_b, conv_norm, w_b_out, w_o, ffn2_norm, ffn2_w_gate, ffn2_w_up, ffn2_w_down, final_norm):
    bsz, seq, _ = x.shape
    depth = w_in.shape[0]
    h = x.reshape(bsz * seq, D_MODEL)
    fw = final_norm.reshape(1, D_MODEL)
    n_ab = 2 * N_HEADS
    c_ab = 4 * KEY_WIDTH
    for l in range(depth):
        h = _ffn(h, ffn1_norm[l].reshape(1, -1), ffn1_w_gate[l].astype(BF16), ffn1_w_up[l].astype(BF16),
                 ffn1_w_down[l].astype(BF16), fw, final=False)
        w_main = jnp.concatenate([w_in[l][:, :c_ab], w_in[l][:, c_ab + n_ab:]], axis=1).astype(BF16)
        w_ab = jnp.pad(w_in[l][:, c_ab:c_ab + n_ab], ((0, 0), (0, LANES - n_ab))).astype(BF16)
        q, k, v, sg, gb, gct, c, gates = _inproj(
            h, mix_norm[l].reshape(1, -1), w_main, w_ab, qkv_conv_w[l],
            _pad_lanes(a_log[l]), _pad_lanes(dt_bias[l]), seq)
        oa, cb = _deltanet(q, k, v, sg, gb, gct, head_norm[l].reshape(1, -1), c, dw_conv_w[l],
                           dw_conv_b[l].reshape(1, -1), conv_norm[l].reshape(1, -1), bsz, seq)
        h = _outproj(h, oa, cb, gates, w_a_out[l].astype(BF16), w_b_out[l].astype(BF16), w_o[l].astype(BF16))
        h = _ffn(h, ffn2_norm[l].reshape(1, -1), ffn2_w_gate[l].astype(BF16), ffn2_w_up[l].astype(BF16),
                 ffn2_w_down[l].astype(BF16), fw, final=(l == depth - 1))
    return h.reshape(bsz, seq, D_MODEL)
```

```python
import functools

import jax
import jax.numpy as jnp
from jax import lax
from jax.experimental import pallas as pl
from jax.experimental.pallas import tpu as pltpu

D_MODEL = 1024
N_HEADS = 8
HEAD_DIM = 128
KEY_WIDTH = N_HEADS * HEAD_DIM
CHUNK = 64
SHORT_CONV = 4
DW_WIDTH = 31
D_FF = 2816
EPS = 1e-6

LANES = 128
SUBLANES = 8
VMEM_LIMIT = 56 * 1024 * 1024

F32 = jnp.float32
BF16 = jnp.bfloat16


def _rms(x, w):
    return x * lax.rsqrt(jnp.mean(x * x, axis=-1, keepdims=True) + EPS) * w


def _sigmoid(x):
    return 1.0 / (1.0 + jnp.exp(-x))


def _silu(x):
    return x * _sigmoid(x)


def _softplus(x):
    return jnp.maximum(x, 0.0) + jnp.log1p(jnp.exp(-jnp.abs(x)))


def _dot(a, b):
    return jnp.dot(a, b, preferred_element_type=F32)


def _dot_nt(a, b):
    return lax.dot_general(a, b, (((1,), (1,)), ((), ())), preferred_element_type=F32)


def _params(sem):
    return pltpu.CompilerParams(dimension_semantics=sem, vmem_limit_bytes=VMEM_LIMIT)


def _resident(shape):
    nd = len(shape)
    return pl.BlockSpec(shape, lambda *_: (0,) * nd, pipeline_mode=pl.Buffered(1))


FFN_TM = 1024
FFN_CW = 256


def _ffn_kernel(x_ref, nw_ref, wg_ref, wu_ref, wd_ref, fw_ref, o_ref, hn_s, a_s, *, final):
    hn_s[...] = _rms(x_ref[...], nw_ref[...]).astype(BF16)
    for c0 in range(0, D_FF, FFN_CW):
        hn = hn_s[...]
        g = _dot(hn, wg_ref[:, c0:c0 + FFN_CW])
        u = _dot(hn, wu_ref[:, c0:c0 + FFN_CW])
        a_s[:, c0:c0 + FFN_CW] = (_silu(g) * u).astype(BF16)
    y = x_ref[...] + 0.5 * _dot(a_s[...], wd_ref[...])
    if final:
        y = _rms(y, fw_ref[...])
    o_ref[...] = y


def _ffn(x, nw, wg, wu, wd, fw, final):
    n = x.shape[0]
    tm = min(FFN_TM, n)
    row = pl.BlockSpec((tm, D_MODEL), lambda i: (i, 0))
    return pl.pallas_call(
        functools.partial(_ffn_kernel, final=final),
        grid=(n // tm,),
        in_specs=[row, _resident((1, D_MODEL)), _resident((D_MODEL, D_FF)), _resident((D_MODEL, D_FF)),
                  _resident((D_FF, D_MODEL)), _resident((1, D_MODEL))],
        out_specs=row,
        out_shape=jax.ShapeDtypeStruct((n, D_MODEL), F32),
        scratch_shapes=[pltpu.VMEM((tm, D_MODEL), BF16), pltpu.VMEM((tm, D_FF), BF16)],
        compiler_params=_params(("parallel",)),
        name="ffn",
    )(x, nw, wg, wu, wd, fw)


INPROJ_TM = 512
INPROJ_GATE_ROWS = 256
INPROJ_CW = 256
INPROJ_RB = 128
_C_GOUT = 3 * KEY_WIDTH
_C_VAL = _C_GOUT + KEY_WIDTH
_C_GT = _C_VAL + D_MODEL
_C_GATES = _C_GT + D_MODEL
_C_END = _C_GATES + 2 * D_MODEL


def _split3(x):
    hi = x.astype(BF16)
    r1 = x - hi.astype(F32)
    mid = r1.astype(BF16)
    lo = (r1 - mid.astype(F32)).astype(BF16)
    return hi, mid, lo


def _inproj_kernel(x_ref, nw_ref, w_ref, wab_ref, cw_ref, alog_ref, dtb_ref,
                   q_ref, k_ref, v_ref, sg_ref, gb_ref, gct_ref, c_ref, gates_ref,
                   hn_s, xb, halo_s, *, tiles_per_seq):
    tm = x_ref.shape[0]
    cw = INPROJ_CW
    rb = INPROJ_RB

    @pl.when(pl.program_id(0) % tiles_per_seq == 0)
    def _():
        halo_s[...] = jnp.zeros_like(halo_s)

    hn_s[...] = _rms(x_ref[...], nw_ref[...]).astype(BF16)

    for group, out_ref in enumerate((q_ref, k_ref, v_ref)):
        g0 = group * KEY_WIDTH
        xc = xb.at[group]
        xc[0:SUBLANES, :] = halo_s[:, g0:g0 + KEY_WIDTH]
        xc[SUBLANES:SUBLANES + tm, :] = _dot(hn_s[...], w_ref[:, g0:g0 + KEY_WIDTH])
        halo_s[:, g0:g0 + KEY_WIDTH] = xc[tm:tm + SUBLANES, :]
    for group, out_ref in enumerate((q_ref, k_ref, v_ref)):
        g0 = group * KEY_WIDTH
        xc = xb.at[group]
        for c0 in range(0, KEY_WIDTH, cw):
            for r0 in range(0, tm, rb):
                xs = xc[r0:r0 + rb + SUBLANES, c0:c0 + cw]
                z = cw_ref[0:1, g0 + c0:g0 + c0 + cw] * xs
                for i in range(1, SHORT_CONV):
                    z = pltpu.roll(z, 1, axis=0) + cw_ref[i:i + 1, g0 + c0:g0 + c0 + cw] * xs
                y = _silu(z[SUBLANES:, :])
                for h0 in range(0, cw, HEAD_DIM):
                    yh = y[:, h0:h0 + HEAD_DIM]
                    if group == 0:
                        yh = yh * lax.rsqrt(jnp.sum(yh * yh, axis=-1, keepdims=True) + EPS) * (HEAD_DIM ** -0.5)
                    elif group == 1:
                        yh = yh * lax.rsqrt(jnp.sum(yh * yh, axis=-1, keepdims=True) + EPS)
                    out_ref[r0:r0 + rb, c0 + h0:c0 + h0 + HEAD_DIM] = yh.astype(out_ref.dtype)

    hn = hn_s[...]
    sg_ref[...] = _silu(_dot(hn, w_ref[:, _C_GOUT:_C_GOUT + KEY_WIDTH])).astype(sg_ref.dtype)

    ab = _dot(hn, wab_ref[...])
    g = -jnp.exp(alog_ref[...]) * _softplus(ab + dtb_ref[...])
    tg = min(INPROJ_GATE_ROWS, tm)
    r = lax.broadcasted_iota(jnp.int32, (tg, tg), 0)
    c = lax.broadcasted_iota(jnp.int32, (tg, tg), 1)
    tri = jnp.where((r // CHUNK == c // CHUNK) & (c <= r), 1.0, 0.0).astype(BF16)
    lane = lax.broadcasted_iota(jnp.int32, (tg, LANES), 1)
    for r0 in range(0, tm, tg):
        g_hi, g_mid, g_lo = _split3(g[r0:r0 + tg, :])
        gc = _dot(tri, g_hi) + _dot(tri, g_mid) + _dot(tri, g_lo)
        gb_ref[r0:r0 + tg, :] = jnp.where(lane < N_HEADS, gc, _sigmoid(ab[r0:r0 + tg, :]))
        gct = gc.T
        for p in range(tg // LANES):
            seg = gct[0:N_HEADS, p * LANES:(p + 1) * LANES]
            pair = (r0 + p * LANES) // CHUNK
            gct_ref[pair] = seg
            gct_ref[pair + 1] = pltpu.roll(seg, CHUNK, axis=1)

    val = _dot(hn, w_ref[:, _C_VAL:_C_VAL + D_MODEL])
    gt = _dot(hn, w_ref[:, _C_GT:_C_GT + D_MODEL])
    c_ref[...] = (val * _sigmoid(gt)).astype(c_ref.dtype)
    for c0 in range(0, 2 * D_MODEL, D_MODEL):
        gates_ref[:, c0:c0 + D_MODEL] = _dot(hn, w_ref[:, _C_GATES + c0:_C_GATES + c0 + D_MODEL]).astype(gates_ref.dtype)


def _inproj(x, nw, w_main, w_ab, conv_w, alog_pad, dtb_pad, seq):
    n = x.shape[0]
    tm = min(INPROJ_TM, seq)
    nch = tm // CHUNK
    row = lambda width: pl.BlockSpec((tm, width), lambda i: (i, 0))
    return pl.pallas_call(
        functools.partial(_inproj_kernel, tiles_per_seq=seq // tm),
        grid=(n // tm,),
        in_specs=[row(D_MODEL), _resident((1, D_MODEL)), _resident((D_MODEL, _C_END)),
                  _resident((D_MODEL, LANES)), _resident((SHORT_CONV, 3 * KEY_WIDTH)),
                  _resident((1, LANES)), _resident((1, LANES))],
        out_specs=[row(KEY_WIDTH), row(KEY_WIDTH), row(KEY_WIDTH), row(KEY_WIDTH), row(LANES),
                   pl.BlockSpec((nch, N_HEADS, LANES), lambda i: (i, 0, 0)),
                   row(D_MODEL), row(2 * D_MODEL)],
        out_shape=[
            jax.ShapeDtypeStruct((n, KEY_WIDTH), BF16),
            jax.ShapeDtypeStruct((n, KEY_WIDTH), BF16),
            jax.ShapeDtypeStruct((n, KEY_WIDTH), BF16),
            jax.ShapeDtypeStruct((n, KEY_WIDTH), BF16),
            jax.ShapeDtypeStruct((n, LANES), F32),
            jax.ShapeDtypeStruct((n // CHUNK, N_HEADS, LANES), F32),
            jax.ShapeDtypeStruct((n, D_MODEL), BF16),
            jax.ShapeDtypeStruct((n, 2 * D_MODEL), BF16),
        ],
        scratch_shapes=[pltpu.VMEM((tm, D_MODEL), BF16),
                        pltpu.VMEM((3, tm + SUBLANES, KEY_WIDTH), F32),
                        pltpu.VMEM((SUBLANES, 3 * KEY_WIDTH), F32)],
        compiler_params=_params(("arbitrary",)),
        name="inproj",
    )(x, nw, w_main, w_ab, conv_w, alog_pad, dtb_pad)


DN_TB = 512
DN_PA = 8
DN_G = 4


def _column(arr, lane):
    ids = lax.broadcasted_iota(jnp.int32, arr.shape, 1)
    return jnp.sum(jnp.where(ids == lane, arr, 0.0), axis=-1, keepdims=True)


CM_HALO = 32
CM_RB = 64


def _dwconv_rows(xbuf, w_ref, b_ref, ybuf, row0, nrows):
    base = CM_HALO - (DW_WIDTH - 1)
    span = CM_RB + CM_HALO
    for l0 in range(0, D_MODEL, LANES):
        for r0 in range(0, nrows, CM_RB):
            rs = pl.multiple_of(row0 + r0, CM_RB)
            xs = xbuf[pl.ds(rs, span), l0:l0 + LANES]
            acc = jnp.zeros((CM_RB, LANES), F32) + b_ref[:, l0:l0 + LANES]
            for phase in range(SUBLANES):
                taps = [i for i in range(DW_WIDTH) if (base + i) % SUBLANES == phase]
                if not taps:
                    continue
                xr = xs if phase == 0 else pltpu.roll(xs, span - phase, axis=0)
                for i in taps:
                    a0 = (base + i) - phase
                    acc = acc + w_ref[i:i + 1, l0:l0 + LANES] * xr[a0:a0 + CM_RB, :]
            ybuf[pl.ds(rs, CM_RB), l0:l0 + LANES] = acc


def _deltanet_kernel(q_ref, k_ref, v_ref, sg_ref, gb_ref, gct_ref, hn_ref, c_ref, cw_ref, cbias_ref, cn_ref,
                     o_ref, cb_ref, state_s, lhs_s, bm_s, ou_s, eg_s, xbuf, ybuf):
    tb = q_ref.shape[0]
    nch = tb // CHUNK

    @pl.when(pl.program_id(1) == 0)
    def _():
        state_s[...] = jnp.zeros_like(state_s)
        xbuf[0:CM_HALO, :] = jnp.zeros((CM_HALO, D_MODEL), F32)

    xbuf[CM_HALO:CM_HALO + tb, :] = c_ref[...].astype(F32)

    hnw = hn_ref[...]
    gw = DN_G * CHUNK
    gl = DN_G * HEAD_DIM
    prow = lax.broadcasted_iota(jnp.int32, (CHUNK, gw), 0)
    plane = lax.broadcasted_iota(jnp.int32, (CHUNK, gw), 1)
    pcol = plane & (CHUNK - 1)
    pgrp = [plane // CHUNK == j for j in range(DN_G)]
    incl = prow >= pcol
    strict = prow > pcol
    eye = jnp.where(prow == pcol, 1.0, 0.0)
    fgrp = [lax.broadcasted_iota(jnp.int32, (CHUNK, gl), 1) // HEAD_DIM == j for j in range(DN_G)]
    lane_lo = lax.broadcasted_iota(jnp.int32, (1, LANES), 1) < CHUNK

    def head_lanes(h):
        return slice(h * HEAD_DIM, (h + 1) * HEAD_DIM)

    def per_head(cols):
        out = cols[DN_G - 1]
        for j in range(DN_G - 2, -1, -1):
            out = jnp.where(pgrp[j], cols[j], out)
        return out

    def block_diag(x, masks):
        return jnp.concatenate([jnp.where(m, x, jnp.zeros_like(x)) for m in masks], axis=0)

    def phase_a(it, carry):
        _dwconv_rows(xbuf, cw_ref, cbias_ref, ybuf, it * (DN_PA * CHUNK), DN_PA * CHUNK)
        probs = [(it * DN_PA + cc, g) for cc in range(DN_PA) for g in range(N_HEADS // DN_G)]
        pre = []
        for ch, g in probs:
            rows = pl.ds(pl.multiple_of(ch * CHUNK, CHUNK), CHUNK)
            gb_blk = gb_ref[rows, :]
            lanes = slice(g * gl, (g + 1) * gl)
            q4 = q_ref[rows, lanes]
            k4 = k_ref[rows, lanes]
            k4f = k4.astype(F32)
            heads = [g * DN_G + j for j in range(DN_G)]
            beta = [_column(gb_blk, N_HEADS + h) for h in heads]
            gcc = [_column(gb_blk, h) for h in heads]
            g_last = [c[CHUNK - 1:CHUNK, :] for c in gcc]
            for h, gl_h in zip(heads, g_last):
                eg_s[ch * N_HEADS + h] = jnp.broadcast_to(jnp.exp(gl_h), (SUBLANES, LANES))
            wide = lambda cols: jnp.concatenate([jnp.broadcast_to(c, (CHUNK, HEAD_DIM)) for c in cols], axis=1)
            beta_x = wide(beta)
            egc_x = wide([jnp.exp(c) for c in gcc])
            ekd_x = wide([jnp.exp(gl_h - c) for gl_h, c in zip(g_last, gcc)])
            gct = gct_ref[ch]
            gct_hi = pltpu.roll(gct, CHUNK, axis=1)
            pair = lambda a: jnp.where(lane_lo, gct[a:a + 1, :], gct_hi[a + 1:a + 2, :])
            gcr = jnp.concatenate([pair(heads[0]), pair(heads[2])], axis=1)
            e_incl = jnp.exp(jnp.where(incl, per_head(gcc) - gcr, -jnp.inf))
            kb = k4f * beta_x
            vb16 = (v_ref[rows, lanes].astype(F32) * beta_x).astype(BF16)
            kbg16 = (kb * egc_x).astype(BF16)
            rhs = jnp.concatenate(
                [jnp.concatenate([vb16[:, head_lanes(j)], kbg16[:, head_lanes(j)]], axis=1) for j in range(DN_G)],
                axis=0)
            pre.append(dict(
                lhs_kq=jnp.concatenate([kb.astype(BF16), q4], axis=0),
                kbd=block_diag(k4, fgrp), e_incl=e_incl, rhs=rhs,
                qd=q4.astype(F32) * egc_x, kdbd=block_diag((k4f * ekd_x).astype(BF16), fgrp)))

        kqs = [_dot_nt(d["lhs_kq"], d["kbd"]) for d in pre]
        ms = [kq[0:CHUNK, :] * jnp.where(strict, d["e_incl"], 0.0) for kq, d in zip(kqs, pre)]
        attns = [(kq[CHUNK:2 * CHUNK, :] * d["e_incl"]).astype(BF16) for kq, d in zip(kqs, pre)]

        p16s = [(-m).astype(BF16) for m in ms]
        tinvs = [eye - m for m in ms]
        pbds = [block_diag(p16, pgrp) for p16 in p16s]
        for _ in range(5):
            p16s = [_dot(p16, pbd).astype(BF16) for p16, pbd in zip(p16s, pbds)]
            pbds = [block_diag(p16, pgrp) for p16 in p16s]
            tinvs = [tv + _dot(tv.astype(BF16), pbd) for tv, pbd in zip(tinvs, pbds)]

        uws = [_dot(block_diag(tv.astype(BF16), pgrp), d["rhs"]).astype(BF16)
               for tv, d in zip(tinvs, pre)]
        auws = [_dot(block_diag(a, pgrp), uw) for a, uw in zip(attns, uws)]
        kuws = [lax.dot_general(d["kdbd"], uw, (((0,), (0,)), ((), ())), preferred_element_type=F32)
                for d, uw in zip(pre, uws)]
        for (ch, g), d, kuw, auw in zip(probs, pre, kuws, auws):
            rows = pl.ds(pl.multiple_of(ch * CHUNK, CHUNK), CHUNK)
            for j in range(DN_G):
                h = g * DN_G + j
                idx = ch * N_HEADS + h
                au = auw[j * CHUNK:(j + 1) * CHUNK, :]
                ku = kuw[j * HEAD_DIM:(j + 1) * HEAD_DIM, :]
                qeff = d["qd"][:, head_lanes(j)] - au[:, HEAD_DIM:2 * HEAD_DIM]
                lhs_s[idx, 0:CHUNK, :] = qeff.astype(BF16)
                lhs_s[idx, CHUNK:CHUNK + HEAD_DIM, :] = ku[:, HEAD_DIM:2 * HEAD_DIM].astype(BF16)
                bm_s[idx] = ku[:, 0:HEAD_DIM]
                ou_s[rows, head_lanes(h)] = au[:, 0:HEAD_DIM]
        return carry

    lax.fori_loop(0, nch // DN_PA, phase_a, 0)

    def phase_b(ch, carry):
        rows = pl.ds(pl.multiple_of(ch * CHUNK, CHUNK), CHUNK)
        res = [_dot(lhs_s[ch * N_HEADS + h], state_s[h].astype(BF16)) for h in range(N_HEADS)]
        for h in range(N_HEADS):
            idx = ch * N_HEADS + h
            state_s[h] = (state_s[h] * eg_s[idx][0:1, :] - res[h][CHUNK:CHUNK + HEAD_DIM, :] + bm_s[idx])
            o = res[h][0:CHUNK, :] + ou_s[rows, head_lanes(h)]
            o = _rms(o, hnw) * sg_ref[rows, head_lanes(h)].astype(F32)
            o_ref[rows, head_lanes(h)] = o.astype(o_ref.dtype)
        cb_ref[rows, :] = _silu(_rms(ybuf[rows, :], cn_ref[...])).astype(cb_ref.dtype)
        return carry

    lax.fori_loop(0, nch, phase_b, 0)
    xbuf[0:CM_HALO, :] = xbuf[tb:tb + CM_HALO, :]


def _deltanet(q, k, v, sg, gb, gct, head_norm, c, cw, cbias, cn, bsz, seq):
    n = q.shape[0]
    tb = min(DN_TB, seq)
    nt = seq // tb
    nch = tb // CHUNK
    assert seq % tb == 0 and nch % DN_PA == 0 and N_HEADS % DN_G == 0
    row = lambda width: pl.BlockSpec((tb, width), lambda b, t: (b * nt + t, 0))
    return pl.pallas_call(
        _deltanet_kernel,
        grid=(bsz, nt),
        in_specs=[row(KEY_WIDTH), row(KEY_WIDTH), row(KEY_WIDTH), row(KEY_WIDTH), row(LANES),
                  pl.BlockSpec((nch, N_HEADS, LANES), lambda b, t: (b * nt + t, 0, 0)),
                  _resident((1, HEAD_DIM)), row(D_MODEL), _resident((DW_WIDTH, D_MODEL)),
                  _resident((1, D_MODEL)), _resident((1, D_MODEL))],
        out_specs=[row(KEY_WIDTH), row(D_MODEL)],
        out_shape=[jax.ShapeDtypeStruct((n, KEY_WIDTH), BF16), jax.ShapeDtypeStruct((n, D_MODEL), BF16)],
        scratch_shapes=[
            pltpu.VMEM((N_HEADS, HEAD_DIM, HEAD_DIM), F32),
            pltpu.VMEM((nch * N_HEADS, CHUNK + HEAD_DIM, HEAD_DIM), BF16),
            pltpu.VMEM((nch * N_HEADS, HEAD_DIM, HEAD_DIM), F32),
            pltpu.VMEM((tb, KEY_WIDTH), F32),
            pltpu.VMEM((nch * N_HEADS, SUBLANES, LANES), F32),
            pltpu.VMEM((tb + CM_HALO, D_MODEL), F32),
            pltpu.VMEM((tb, D_MODEL), F32),
        ],
        compiler_params=_params(("parallel", "arbitrary")),
        name="deltanet",
    )(q, k, v, sg, gb, gct, head_norm, c, cw, cbias, cn)


OUT_TM = 1024


def _outproj_kernel(x_ref, oa_ref, cb_ref, gates_ref, wa_ref, wb_ref, wo_ref, o_ref):
    ya = _dot(oa_ref[...], wa_ref[...])
    yb = _dot(cb_ref[...], wb_ref[...])
    merged = (_sigmoid(gates_ref[:, 0:D_MODEL].astype(F32)) * ya
              + _sigmoid(gates_ref[:, D_MODEL:2 * D_MODEL].astype(F32)) * yb)
    o_ref[...] = x_ref[...] + _dot(merged.astype(BF16), wo_ref[...])


def _outproj(x, oa, cb, gates, wa, wb, wo):
    n = x.shape[0]
    tm = min(OUT_TM, n)
    row = lambda width: pl.BlockSpec((tm, width), lambda i: (i, 0))
    wspec = _resident((D_MODEL, D_MODEL))
    return pl.pallas_call(
        _outproj_kernel,
        grid=(n // tm,),
        in_specs=[row(D_MODEL), row(KEY_WIDTH), row(D_MODEL), row(2 * D_MODEL), wspec, wspec, wspec],
        out_specs=row(D_MODEL),
        out_shape=jax.ShapeDtypeStruct((n, D_MODEL), F32),
        compiler_params=_params(("parallel",)),
        name="outproj",
    )(x, oa, cb, gates, wa, wb, wo)


MF_TM = 512


def _mixffn_kernel(x_ref, oa_ref, cb_ref, gates_ref, wa_ref, wb_ref, wo_ref, nw_ref, wg_ref, wu_ref, wd_ref,
                   fw_ref, o_ref, x1_s, hn_s, a_s, *, final):
    ya = _dot(oa_ref[...], wa_ref[...])
    yb = _dot(cb_ref[...], wb_ref[...])
    merged = (_sigmoid(gates_ref[:, 0:D_MODEL].astype(F32)) * ya
              + _sigmoid(gates_ref[:, D_MODEL:2 * D_MODEL].astype(F32)) * yb)
    x1_s[...] = x_ref[...] + _dot(merged.astype(BF16), wo_ref[...])
    hn_s[...] = _rms(x1_s[...], nw_ref[...]).astype(BF16)
    for c0 in range(0, D_FF, FFN_CW):
        hn = hn_s[...]
        g = _dot(hn, wg_ref[:, c0:c0 + FFN_CW])
        u = _dot(hn, wu_ref[:, c0:c0 + FFN_CW])
        a_s[:, c0:c0 + FFN_CW] = (_silu(g) * u).astype(BF16)
    y = x1_s[...] + 0.5 * _dot(a_s[...], wd_ref[...])
    if final:
        y = _rms(y, fw_ref[...])
    o_ref[...] = y


def _mixffn(x, oa, cb, gates, wa, wb, wo, nw, wg, wu, wd, fw, final):
    n = x.shape[0]
    tm = min(MF_TM, n)
    row = lambda width: pl.BlockSpec((tm, width), lambda i: (i, 0))
    wsq = _resident((D_MODEL, D_MODEL))
    return pl.pallas_call(
        functools.partial(_mixffn_kernel, final=final),
        grid=(n // tm,),
        in_specs=[row(D_MODEL), row(KEY_WIDTH), row(D_MODEL), row(2 * D_MODEL), wsq, wsq, wsq,
                  _resident((1, D_MODEL)), _resident((D_MODEL, D_FF)), _resident((D_MODEL, D_FF)),
                  _resident((D_FF, D_MODEL)), _resident((1, D_MODEL))],
        out_specs=row(D_MODEL),
        out_shape=jax.ShapeDtypeStruct((n, D_MODEL), F32),
        scratch_shapes=[pltpu.VMEM((tm, D_MODEL), F32), pltpu.VMEM((tm, D_MODEL), BF16),
                        pltpu.VMEM((tm, D_FF), BF16)],
        compiler_params=_params(("parallel",)),
        name="mixffn",
    )(x, oa, cb, gates, wa, wb, wo, nw, wg, wu, wd, fw)


def _pad_lanes(v):
    return jnp.zeros((1, LANES), F32).at[0, 0:v.shape[0]].set(v.astype(F32))


def kernel(x, ffn1_norm, ffn1_w_gate, ffn1_w_up, ffn1_w_down, mix_norm, w_in, qkv_conv_w, a_log, dt_bias, head_norm, w_a_out, dw_conv_w, dw_conv_b, conv_norm, w_b_out, w_o, ffn2_norm, ffn2_w_gate, ffn2_w_up, ffn2_w_down, final_norm):
    bsz, seq, _ = x.shape
    depth = w_in.shape[0]
    h = x.reshape(bsz * seq, D_MODEL)
    fw = final_norm.reshape(1, D_MODEL)
    n_ab = 2 * N_HEADS
    c_ab = 4 * KEY_WIDTH
    for l in range(depth):
        h = _ffn(h, ffn1_norm[l].reshape(1, -1), ffn1_w_gate[l].astype(BF16), ffn1_w_up[l].astype(BF16),
                 ffn1_w_down[l].astype(BF16), fw, final=False)
        w_main = jnp.concatenate([w_in[l][:, :c_ab], w_in[l][:, c_ab + n_ab:]], axis=1).astype(BF16)
        w_ab = jnp.pad(w_in[l][:, c_ab:c_ab + n_ab], ((0, 0), (0, LANES - n_ab))).astype(BF16)
        q, k, v, sg, gb, gct, c, gates = _inproj(
            h, mix_norm[l].reshape(1, -1), w_main, w_ab, qkv_conv_w[l],
            _pad_lanes(a_log[l]), _pad_lanes(dt_bias[l]), seq)
        oa, cb = _deltanet(q, k, v, sg, gb, gct, head_norm[l].reshape(1, -1), c, dw_conv_w[l],
                           dw_conv_b[l].reshape(1, -1), conv_norm[l].reshape(1, -1), bsz, seq)
        h = _mixffn(h, oa, cb, gates, w_a_out[l].astype(BF16), w_b_out[l].astype(BF16), w_o[l].astype(BF16),
                    ffn2_norm[l].reshape(1, -1), ffn2_w_gate[l].astype(BF16), ffn2_w_up[l].astype(BF16),
                    ffn2_w_down[l].astype(BF16), fw, final=(l == depth - 1))
    return h.reshape(bsz, seq, D_MODEL)
```
